```python
import math
import jax
import jax.numpy as jnp
from jax import lax
import numpy as np


D_MODEL = 1024
BATCH = 16
SEQ = 2048
DEPTH = 2
DEC_BATCH = 32
DEC_SEQ = 8
PAST_LEN = 16384
PAGE_SIZE = 128

N_A_LAYERS = max(DEPTH // 2, 1)
N_B_LAYERS = DEPTH - N_A_LAYERS
TOK_WIDTH = (3 * D_MODEL) // 4
MEM_WIDTH = D_MODEL - TOK_WIDTH
MEM_HEADS = 4
MEM_HEAD_DIM = MEM_WIDTH // MEM_HEADS
N_MEM = 256
CONV_WIDTH = 3
DIFF_HEAD_DIM = 64
DIFF_HEADS = TOK_WIDTH // (2 * DIFF_HEAD_DIM)
DIFF_V_DIM = 2 * DIFF_HEAD_DIM
N_BUCKETS = 32
MAX_DISTANCE = 128
PEER_HEADS = 8
PEER_KEYS = 128
PEER_EXPERTS = PEER_KEYS * PEER_KEYS
PEER_QDIM = 256
PEER_TOPK = 16
PEER_BLOCK = 256
Q_BLOCK = 128
EPS = 1e-6
NEG = -1e30

kernel_name = 'yoco_shortconv_diffattn_peer_decoder_step'


def rmsnorm(x, g):
    x32 = x.astype(jnp.float32)
    y = x32 * lax.rsqrt(jnp.mean(x32 * x32, axis=-1, keepdims=True) + EPS)
    return y.astype(x.dtype) * g


def rel_bucket(qpos, kpos):
    n = jnp.maximum(qpos - kpos, 0)
    max_exact = N_BUCKETS // 2
    large = max_exact + (jnp.log(jnp.maximum(n, 1).astype(jnp.float32) / max_exact)
                         / math.log(MAX_DISTANCE / max_exact) * (N_BUCKETS - max_exact)).astype(jnp.int32)
    large = jnp.minimum(large, N_BUCKETS - 1)
    return jnp.where(n < max_exact, n, large)


def memory_kv(mem, g_mem, w_mem_kv, g_kn):
    b, m, _ = mem.shape
    z = rmsnorm(mem, g_mem) @ w_mem_kv
    k = rmsnorm(z[..., :MEM_WIDTH].reshape(b, m, MEM_HEADS, MEM_HEAD_DIM), g_kn)
    v = z[..., MEM_WIDTH:].reshape(b, m, MEM_HEADS, MEM_HEAD_DIM)
    return k, v


def mem_attention(qm, mem_k, mem_v, g_qn):
    b, s, _ = qm.shape
    q = rmsnorm(qm.reshape(b, s, MEM_HEADS, MEM_HEAD_DIM), g_qn)
    sc = jnp.einsum('bqhd,bmhd->bhqm', q, mem_k).astype(jnp.float32) * (MEM_HEAD_DIM ** -0.5)
    p = jax.nn.softmax(sc, axis=-1).astype(mem_v.dtype)
    return jnp.einsum('bhqm,bmhd->bqhd', p, mem_v).reshape(b, s, MEM_WIDTH)


def shared_kv(h, g_kv, w_kv, g_kn):
    b, s, _ = h.shape
    z = rmsnorm(h, g_kv) @ w_kv
    k = rmsnorm(z[..., :TOK_WIDTH].reshape(b, s, DIFF_HEADS, 2, DIFF_HEAD_DIM), g_kn)
    v = z[..., TOK_WIDTH:].reshape(b, s, DIFF_HEADS, DIFF_V_DIM)
    return k, v


def diff_attn_core(q, ks, vs, qpos, kpos, lam, rel_bias):
    s = jnp.concatenate([jnp.einsum('bqhcd,bkhcd->bhcqk', q, k) for k in ks], axis=-1)
    s = s.astype(jnp.float32) * (DIFF_HEAD_DIM ** -0.5)
    bias = jnp.transpose(rel_bias[rel_bucket(qpos[:, None], kpos[None, :])], (2, 0, 1))
    s = s + bias.astype(jnp.float32)[None, :, None]
    s = jnp.where(kpos[None, :] <= qpos[:, None], s, NEG)
    p = jax.nn.softmax(s, axis=-1)
    a = (p[:, :, 0] - lam * p[:, :, 1]).astype(vs[0].dtype)
    o = None
    off = 0
    for v in vs:
        n = v.shape[1]
        t = jnp.einsum('bhqk,bkhe->bqhe', a[..., off:off + n], v)
        o = t if o is None else o + t
        off += n
    return o


def prompt_attend(q, k, v, lam, rel_bias):
    b, s = q.shape[0], q.shape[1]
    nb = s // Q_BLOCK
    qb = q.reshape(b, nb, Q_BLOCK, DIFF_HEADS, 2, DIFF_HEAD_DIM).swapaxes(0, 1)
    kpos = jnp.arange(s, dtype=jnp.int32)

    def blk(args):
        qi, start = args
        qpos = start + jnp.arange(Q_BLOCK, dtype=jnp.int32)
        return diff_attn_core(qi, (k,), (v,), qpos, kpos, lam, rel_bias)

    o = lax.map(blk, (qb, jnp.arange(nb, dtype=jnp.int32) * Q_BLOCK))
    return o.swapaxes(0, 1).reshape(b, s, DIFF_HEADS, DIFF_V_DIM)


def make_sample_attend(cache_k, cache_v, page_table):
    db, n_pages = page_table.shape
    past = n_pages * cache_k.shape[1]
    kp = cache_k[page_table].reshape(db, past, DIFF_HEADS, 2, DIFF_HEAD_DIM)
    vp = cache_v[page_table].reshape(db, past, DIFF_HEADS, DIFF_V_DIM)

    def attend(q, k, v, lam, rel_bias):
        sq = q.shape[1]
        qpos = past + jnp.arange(sq, dtype=jnp.int32)
        kpos = jnp.arange(past + sq, dtype=jnp.int32)
        return diff_attn_core(q, (kp, k), (vp, v), qpos, kpos, lam, rel_bias)

    return attend


def peer_block(xb, w_q, sub_keys, u, v):
    n = xb.shape[0]
    q = (xb @ w_q).reshape(n, PEER_HEADS, 2, PEER_QDIM // 2)
    sc = jnp.einsum('nhcd,ckd->nhck', q, sub_keys).astype(jnp.float32)
    sv, si = lax.top_k(sc, PEER_TOPK)
    cand = (sv[:, :, 0, :, None] + sv[:, :, 1, None, :]).reshape(n, PEER_HEADS, PEER_TOPK * PEER_TOPK)
    cidx = (si[:, :, 0, :, None] * PEER_KEYS + si[:, :, 1, None, :]).reshape(n, PEER_HEADS, PEER_TOPK * PEER_TOPK)
    top, pos = lax.top_k(cand, PEER_TOPK)
    e = jnp.take_along_axis(cidx, pos, axis=-1)
    g = jax.nn.softmax(top, axis=-1)
    act = jax.nn.gelu(jnp.einsum('nd,nhkd->nhk', xb, u[e]).astype(jnp.float32), approximate=False)
    return jnp.einsum('nhk,nhkd->nd', (g * act).astype(xb.dtype), v[e])


def peer_ffn(x, w_q, sub_keys, u, v):
    b, s, d = x.shape
    n = b * s
    nb = -(-n // PEER_BLOCK)
    xt = jnp.pad(x.reshape(n, d), ((0, nb * PEER_BLOCK - n), (0, 0))).reshape(nb, PEER_BLOCK, d)
    out = lax.map(lambda xb: peer_block(xb, w_q, sub_keys, u, v), xt)
    return out.reshape(nb * PEER_BLOCK, d)[:n].reshape(b, s, d)


def trunk(h, conv_prev, mem_k, mem_v, attend, P):
    b, s, _ = h.shape
    new_conv = []
    k_sh = None
    v_sh = None
    for l in range(DEPTH):
        xn = rmsnorm(h, P['g_mix'][l])
        if l < N_A_LAYERS:
            a = l
            z = xn @ P['w_in_a'][a]
            b_gate = z[..., :TOK_WIDTH]
            c_gate = z[..., TOK_WIDTH:2 * TOK_WIDTH]
            hh = z[..., 2 * TOK_WIDTH:3 * TOK_WIDTH]
            qm = z[..., 3 * TOK_WIDTH:]
            u = c_gate * hh
            full = jnp.concatenate([conv_prev[a].astype(u.dtype), u], axis=1)
            w = P['w_conv'][a]
            conv = w[0] * full[:, 0:s] + w[1] * full[:, 1:s + 1] + w[2] * full[:, 2:s + 2]
            tok = b_gate * conv
            new_conv.append(full[:, s:])
            w_out = P['w_out_a'][a]
        else:
            bi = l - N_A_LAYERS
            z = xn @ P['w_in_b'][bi]
            q = rmsnorm(z[..., :TOK_WIDTH].reshape(b, s, DIFF_HEADS, 2, DIFF_HEAD_DIM), P['g_qn_b'][bi])
            qm = z[..., TOK_WIDTH:]
            lam_init = 0.8 - 0.6 * math.exp(-0.3 * l)
            lp = P['lam_b'][bi].astype(jnp.float32)
            lam = jnp.exp(jnp.sum(lp[0] * lp[1])) - jnp.exp(jnp.sum(lp[2] * lp[3])) + lam_init
            o = attend(q, k_sh, v_sh, lam, P['rel_bias'])
            tok = (rmsnorm(o, P['g_subln'][bi]) * (1.0 - lam_init)).reshape(b, s, TOK_WIDTH)
            w_out = P['w_out_b'][bi]
        mo = mem_attention(qm, mem_k[l], mem_v[l], P['g_mem_qn'][l])
        h = h + jnp.concatenate([tok, mo], axis=-1) @ w_out
        h = h + peer_ffn(rmsnorm(h, P['g_ffn'][l]), P['w_peer_q'][l], P['peer_subkeys'][l],
                         P['peer_u'][l], P['peer_v'][l])
        if l == N_A_LAYERS - 1:
            k_sh, v_sh = shared_kv(h, P['g_kv'], P['w_kv'], P['g_kn'])
    return h, jnp.stack(new_conv), k_sh, v_sh


def setup_inputs(seed: int = 0) -> dict:
    key = jax.random.key(seed)
    ks = jax.random.split(key, 40)
    f32 = jnp.float32

    def nrm(k, shape, scale):
        return jax.random.normal(k, shape, f32) * scale

    def gain(k, shape):
        return 1.0 + 0.02 * jax.random.normal(k, shape, f32)

    n_pages = PAST_LEN // PAGE_SIZE
    n_used = DEC_BATCH * n_pages
    n_pool = (n_used * 5) // 4
    page_table = jax.random.permutation(ks[0], n_pool)[:n_used].reshape(DEC_BATCH, n_pages).astype(jnp.int32)
    d = D_MODEL
    return {
        'x_prompt': nrm(ks[1], (BATCH, SEQ, d), 1.0),
        'x_sample': nrm(ks[2], (DEC_BATCH, DEC_SEQ, d), 1.0),
        'state_conv': nrm(ks[3], (N_A_LAYERS, DEC_BATCH, CONV_WIDTH - 1, TOK_WIDTH), 1.0),
        'cache_k': nrm(ks[4], (n_pool, PAGE_SIZE, DIFF_HEADS, 2, DIFF_HEAD_DIM), 1.0),
        'cache_v': nrm(ks[5], (n_pool, PAGE_SIZE, DIFF_HEADS, DIFF_V_DIM), 1.0),
        'cache_mem_k': nrm(ks[6], (DEPTH, DEC_BATCH, N_MEM, MEM_HEADS, MEM_HEAD_DIM), 1.0),
        'cache_mem_v': nrm(ks[7], (DEPTH, DEC_BATCH, N_MEM, MEM_HEADS, MEM_HEAD_DIM), 1.0),
        'page_table': page_table,
        'mem_prompt': nrm(ks[8], (BATCH, N_MEM, d), 1.0),
        'g_mix': gain(ks[9], (DEPTH, d)),
        'g_ffn': gain(ks[10], (DEPTH, d)),
        'w_in_a': nrm(ks[11], (N_A_LAYERS, d, 3 * TOK_WIDTH + MEM_WIDTH), d ** -0.5),
        'w_conv': nrm(ks[12], (N_A_LAYERS, CONV_WIDTH, TOK_WIDTH), 0.5),
        'w_out_a': nrm(ks[13], (N_A_LAYERS, TOK_WIDTH + MEM_WIDTH, d), (TOK_WIDTH + MEM_WIDTH) ** -0.5),
        'w_in_b': nrm(ks[14], (N_B_LAYERS, d, TOK_WIDTH + MEM_WIDTH), d ** -0.5),
        'g_qn_b': gain(ks[15], (N_B_LAYERS, DIFF_HEAD_DIM)),
        'lam_b': nrm(ks[16], (N_B_LAYERS, 4, DIFF_HEAD_DIM), 0.1),
        'g_subln': gain(ks[17], (N_B_LAYERS, DIFF_V_DIM)),
        'w_out_b': nrm(ks[18], (N_B_LAYERS, TOK_WIDTH + MEM_WIDTH, d), (TOK_WIDTH + MEM_WIDTH) ** -0.5),
        'g_kv': gain(ks[19], (d,)),
        'w_kv': nrm(ks[20], (d, TOK_WIDTH + DIFF_HEADS * DIFF_V_DIM), d ** -0.5),
        'g_kn': gain(ks[21], (DIFF_HEAD_DIM,)),
        'rel_bias': nrm(ks[22], (N_BUCKETS, DIFF_HEADS), 0.5),
        'g_mem': gain(ks[23], (DEPTH, d)),
        'w_mem_kv': nrm(ks[24], (DEPTH, d, 2 * MEM_WIDTH), d ** -0.5),
        'g_mem_qn': gain(ks[25], (DEPTH, MEM_HEAD_DIM)),
        'g_mem_kn': gain(ks[26], (DEPTH, MEM_HEAD_DIM)),
        'w_peer_q': nrm(ks[27], (DEPTH, d, PEER_HEADS * PEER_QDIM), d ** -0.5),
        'peer_subkeys': nrm(ks[28], (DEPTH, 2, PEER_KEYS, PEER_QDIM // 2), (PEER_QDIM // 2) ** -0.5),
        'peer_u': nrm(ks[29], (DEPTH, PEER_EXPERTS, d), d ** -0.5),
        'peer_v': nrm(ks[30], (DEPTH, PEER_EXPERTS, d), PEER_HEADS ** -0.5),
    }


def reference(x_prompt, x_sample, state_conv, cache_k, cache_v, cache_mem_k, cache_mem_v, page_table,
              mem_prompt, g_mix, g_ffn, w_in_a, w_conv, w_out_a, w_in_b, g_qn_b, lam_b, g_subln, w_out_b,
              g_kv, w_kv, g_kn, rel_bias, g_mem, w_mem_kv, g_mem_qn, g_mem_kn, w_peer_q, peer_subkeys,
              peer_u, peer_v):
    P = {'g_mix': g_mix, 'g_ffn': g_ffn, 'w_in_a': w_in_a, 'w_conv': w_conv, 'w_out_a': w_out_a,
         'w_in_b': w_in_b, 'g_qn_b': g_qn_b, 'lam_b': lam_b, 'g_subln': g_subln, 'w_out_b': w_out_b,
         'g_kv': g_kv, 'w_kv': w_kv, 'g_kn': g_kn, 'rel_bias': rel_bias, 'g_mem_qn': g_mem_qn,
         'w_peer_q': w_peer_q, 'peer_subkeys': peer_subkeys, 'peer_u': peer_u, 'peer_v': peer_v}

    mk_list = []
    mv_list = []
    for l in range(DEPTH):
        mk, mv = memory_kv(mem_prompt, g_mem[l], w_mem_kv[l], g_mem_kn[l])
        mk_list.append(mk)
        mv_list.append(mv)
    mem_k_prompt = jnp.stack(mk_list)
    mem_v_prompt = jnp.stack(mv_list)
    conv0 = jnp.zeros((N_A_LAYERS, x_prompt.shape[0], CONV_WIDTH - 1, TOK_WIDTH), x_prompt.dtype)
    y_prompt, conv_prompt, k_prompt, v_prompt = trunk(x_prompt, conv0, mem_k_prompt, mem_v_prompt,
                                                      prompt_attend, P)

    attend_s = make_sample_attend(cache_k, cache_v, page_table)
    y_sample, conv_sample, k_sample, v_sample = trunk(x_sample, state_conv, cache_mem_k, cache_mem_v,
                                                      attend_s, P)
    return (y_prompt, y_sample, conv_prompt, conv_sample, k_prompt, v_prompt, k_sample, v_sample,
            mem_k_prompt, mem_v_prompt)
```

```python
import functools
import math

import numpy as np
import jax
import jax.numpy as jnp
from jax import lax
from jax.experimental import pallas as pl
from jax.experimental.pallas import tpu as pltpu

F32 = jnp.float32
BF16 = jnp.bfloat16

EPS = 1e-6
NEG = -1e30
LANES = 128
GROUP = 64
MEM_HEADS = 4
DIFF_HEADS = 6
N_BUCKETS = 32
MAX_DISTANCE = 128
PEER_HEADS = 8
PEER_KEYS = 128
PEER_TOPK = 16
VMEM_LIMIT = 48 * 1024 * 1024


def _cparams(n_grid):
    return pltpu.CompilerParams(dimension_semantics=("arbitrary",) * n_grid,
                                vmem_limit_bytes=VMEM_LIMIT)


def _dot(a, b):
    return jnp.dot(a, b, preferred_element_type=F32)


def _dot_nt(a, b):
    return lax.dot_general(a, b, (((1,), (1,)), ((), ())), preferred_element_type=F32)


def _rms(x, g):
    ms = jnp.mean(x * x, axis=-1, keepdims=True)
    return x * lax.rsqrt(ms + EPS) * g


def _group_rms(x, bd, g):
    x2 = x * x
    hi = x2.astype(BF16)
    lo = (x2 - hi.astype(F32)).astype(BF16)
    parts = []
    for s in range(x.shape[-1] // 256):
        sl = slice(s * 256, (s + 1) * 256)
        parts.append(_dot(hi[:, sl], bd) + _dot(lo[:, sl], bd))
    ms = parts[0] if len(parts) == 1 else jnp.concatenate(parts, axis=-1)
    return x * lax.rsqrt(ms + EPS) * g


def _mem_attn(qm, mk_t, mv_t, bd, gq, hmask):
    q = _group_rms(qm, bd, gq) * (GROUP ** -0.5)
    out = None
    for hd in range(MEM_HEADS):
        msk = hmask[hd:hd + 1, :]
        s = _dot((q * msk).astype(BF16), mk_t)
        m = jnp.max(s, axis=-1, keepdims=True)
        p = jnp.exp(s - m)
        l = jnp.sum(p, axis=-1, keepdims=True)
        t = _dot_nt(p.astype(BF16), mv_t) * (1.0 / l) * msk
        out = t if out is None else out + t
    return out


def _block_diag_mean():
    i = np.arange(256)
    return jnp.asarray((i[:, None] // GROUP == i[None, :] // GROUP).astype(np.float32) / GROUP, BF16)


def _mem_head_mask():
    i = np.arange(MEM_HEADS * GROUP)
    return jnp.asarray((i[None, :] // GROUP == np.arange(MEM_HEADS)[:, None]).astype(np.float32))


def _sub_head_mask():
    i = np.arange(2 * GROUP)
    return jnp.asarray((i[None, :] // GROUP == np.arange(2)[:, None]).astype(np.float32))


def _bucket_np(n):
    n = np.maximum(n, 0)
    max_exact = N_BUCKETS // 2
    ratio = np.log(np.maximum(n, 1).astype(np.float32) / np.float32(max_exact)) / np.float32(
        math.log(MAX_DISTANCE / max_exact))
    large = max_exact + (ratio * np.float32(N_BUCKETS - max_exact)).astype(np.int32)
    large = np.minimum(large, N_BUCKETS - 1)
    return np.where(n < max_exact, n, large).astype(np.int32)


def _mem_kv_kernel(mem_ref, g_ref, w_ref, gk_ref, bd_ref, k_ref, v_ref):
    xn = _rms(mem_ref[0], g_ref[0]).astype(BF16)
    z = _dot(xn, w_ref[0])
    w = z.shape[-1] // 2
    k_ref[0, 0] = _group_rms(z[:, :w], bd_ref[...], gk_ref[0]).T
    v_ref[0, 0] = z[:, w:].T


def _mem_kv(mem, g_mem, w_mem_kv, g_kn, bd):
    b, nm, d = mem.shape
    depth = g_mem.shape[0]
    w = w_mem_kv.shape[-1] // 2
    out = jax.ShapeDtypeStruct((depth, b, w, nm), F32)
    return pl.pallas_call(
        _mem_kv_kernel,
        grid=(depth, b),
        in_specs=[
            pl.BlockSpec((1, nm, d), lambda l, i: (i, 0, 0)),
            pl.BlockSpec((1, 1, d), lambda l, i: (l, 0, 0)),
            pl.BlockSpec((1, d, 2 * w), lambda l, i: (l, 0, 0)),
            pl.BlockSpec((1, 1, w), lambda l, i: (l, 0, 0)),
            pl.BlockSpec((256, 256), lambda l, i: (0, 0)),
        ],
        out_specs=[pl.BlockSpec((1, 1, w, nm), lambda l, i: (l, i, 0, 0))] * 2,
        out_shape=[out, out],
        compiler_params=_cparams(2),
        name="mem_kv",
    )(mem, g_mem.reshape(depth, 1, d), w_mem_kv.astype(BF16),
      jnp.tile(g_kn, (1, MEM_HEADS)).reshape(depth, 1, w), bd)


def _layer_a_kernel(h_ref, prev_ref, mk_ref, mv_ref, gmix_ref, win_ref, wconv_ref, wout_ref, gq_ref,
                    bd_ref, hmask_ref, out_ref, conv_ref, ubuf):
    j = pl.program_id(1)
    ts = h_ref.shape[1]
    tw = wconv_ref.shape[1]
    h = h_ref[0]
    xn = _rms(h, gmix_ref[...]).astype(BF16)
    z = _dot(xn, win_ref[...])
    u = z[:, tw:2 * tw] * z[:, 2 * tw:3 * tw]

    @pl.when(j == 0)
    def _():
        ubuf[6:8, :] = prev_ref[0]

    @pl.when(j > 0)
    def _():
        ubuf[6:8, :] = ubuf[ts + 6:ts + 8, :]

    ubuf[8:8 + ts, :] = u
    conv = (wconv_ref[0:1, :] * ubuf[6:6 + ts, :] + wconv_ref[1:2, :] * ubuf[7:7 + ts, :]
            + wconv_ref[2:3, :] * u)
    conv_ref[0] = ubuf[ts + 6:ts + 8, :]
    tok = z[:, :tw] * conv
    mo = _mem_attn(z[:, 3 * tw:], mk_ref[0].astype(BF16), mv_ref[0].astype(BF16), bd_ref[...],
                   gq_ref[...], hmask_ref[...])
    out_ref[0] = (h + _dot(tok.astype(BF16), wout_ref[0:tw, :])
                  + _dot(mo.astype(BF16), wout_ref[tw:, :]))


def _layer_a(h, conv_prev, mem_k, mem_v, g_mix, w_in, w_conv, w_out, g_qn, bd, hmask, ts):
    b, s, d = h.shape
    tw = w_conv.shape[1]
    mw = mem_k.shape[1]
    nm = mem_k.shape[2]
    const = lambda i, j: (0, 0)
    return pl.pallas_call(
        _layer_a_kernel,
        grid=(b, s // ts),
        in_specs=[
            pl.BlockSpec((1, ts, d), lambda i, j: (i, j, 0)),
            pl.BlockSpec((1, 2, tw), lambda i, j: (i, 0, 0)),
            pl.BlockSpec((1, mw, nm), lambda i, j: (i, 0, 0)),
            pl.BlockSpec((1, mw, nm), lambda i, j: (i, 0, 0)),
            pl.BlockSpec((1, d), const),
            pl.BlockSpec(w_in.shape, const),
            pl.BlockSpec(w_conv.shape, const),
            pl.BlockSpec(w_out.shape, const),
            pl.BlockSpec((1, mw), const),
            pl.BlockSpec((256, 256), const),
            pl.BlockSpec(hmask.shape, const),
        ],
        out_specs=[pl.BlockSpec((1, ts, d), lambda i, j: (i, j, 0)),
                   pl.BlockSpec((1, 2, tw), lambda i, j: (i, 0, 0))],
        out_shape=[jax.ShapeDtypeStruct((b, s, d), F32), jax.ShapeDtypeStruct((b, 2, tw), F32)],
        scratch_shapes=[pltpu.VMEM((ts + 8, tw), F32)],
        compiler_params=_cparams(2),
        name="layer_a",
    )(h, conv_prev, mem_k, mem_v, g_mix.reshape(1, d), w_in.astype(BF16), w_conv, w_out.astype(BF16),
      jnp.tile(g_qn, MEM_HEADS).reshape(1, mw), bd, hmask)


def _top16_rows(v):
    rows = []
    for r in range(PEER_TOPK):
        m = jnp.max(v, axis=0, keepdims=True)
        rows.append(m)
        if r + 1 < PEER_TOPK:
            v = jnp.where(v == m, -jnp.inf, v)
    return jnp.concatenate(rows, axis=0)


def _peer_pre_kernel(h_ref, g_ref, wq_ref, sk_ref, hnT_ref, s1_ref, e1_ref, tau_ref, e0_ref, sc_scr):
    tb = h_ref.shape[0]
    ncol = tb // LANES
    hn = _rms(h_ref[...], g_ref[...])
    hnT_ref[...] = hn.T.astype(BF16)
    q = _dot(hn.astype(BF16), wq_ref[...]).astype(BF16)
    for hc in range(2 * PEER_HEADS):
        sc_t = _dot_nt(sk_ref[hc % 2], q[:, hc * LANES:(hc + 1) * LANES])
        for col in range(ncol):
            sc_scr[hc, col] = sc_t[:, col * LANES:(col + 1) * LANES]

    def body(idx, carry):
        hd = idx // ncol
        col = idx % ncol
        s0 = sc_scr[2 * hd, col]
        s1 = sc_scr[2 * hd + 1, col]
        a = _top16_rows(s0)
        b = _top16_rows(s1)
        cand = jnp.concatenate(
            [a[0:1] + b] + [a[r:r + 1] + b[0:8] for r in range(1, 8)] + [a[8:16] + b[0:1]], axis=0)
        v = cand
        for _ in range(PEER_TOPK - 1):
            m = jnp.max(v, axis=0, keepdims=True)
            v = jnp.where(v == m, -jnp.inf, v)
        t = jnp.max(v, axis=0, keepdims=True)
        top = a[0:1] + b[0:1]
        zsum = jnp.sum(jnp.where(cand >= t, jnp.exp(cand - top), 0.0), axis=0, keepdims=True)
        tau = jnp.full(s0.shape, jnp.inf, F32)
        for c in range(PEER_TOPK):
            bc = b[c:c + 1]
            tau = jnp.where(s0 + bc >= t, bc, tau)
        tau = jnp.where(s0 >= a[PEER_TOPK - 1:PEER_TOPK], tau, jnp.inf)
        s1_ref[hd, col] = s1
        e1_ref[hd, col] = jnp.exp(s1 - b[0:1]) * (1.0 / zsum)
        tau_ref[hd, col] = tau
        e0_ref[hd, col] = jnp.exp(s0 - a[0:1])
        return carry

    lax.fori_loop(0, PEER_HEADS * ncol, body, 0)


def _peer_pre(h2d, g_ffn, w_q, sub_keys, tb):
    n, d = h2d.shape
    ncols = n // LANES
    ncol = tb // LANES
    const2 = lambda i: (0, 0)
    tile = jax.ShapeDtypeStruct((PEER_HEADS, ncols, PEER_KEYS, LANES), F32)
    tile_spec = pl.BlockSpec((PEER_HEADS, ncol, PEER_KEYS, LANES), lambda i: (0, i, 0, 0))
    return pl.pallas_call(
        _peer_pre_kernel,
        grid=(n // tb,),
        in_specs=[
            pl.BlockSpec((tb, d), lambda i: (i, 0)),
            pl.BlockSpec((1, d), const2),
            pl.BlockSpec(w_q.shape, const2),
            pl.BlockSpec(sub_keys.shape, lambda i: (0, 0, 0)),
        ],
        out_specs=[pl.BlockSpec((d, tb), lambda i: (0, i))] + [tile_spec] * 4,
        out_shape=[jax.ShapeDtypeStruct((d, n), BF16)] + [tile] * 4,
        scratch_shapes=[pltpu.VMEM((2 * PEER_HEADS, ncol, PEER_KEYS, LANES), F32)],
        compiler_params=_cparams(1),
        name="peer_pre",
    )(h2d, g_ffn.reshape(1, d), w_q.astype(BF16), sub_keys.astype(BF16))


def _peer_dense_kernel(hnT_ref, s1_ref, e1_ref, tau_ref, e0_ref, u_ref, vt_ref, h_ref, out_ref,
                       acc, a_scr, p_scr):
    k = pl.program_id(1)
    nsub = a_scr.shape[0]
    ncol = s1_ref.shape[1]
    sqrt_half = np.float32(np.sqrt(0.5))

    @pl.when(k == 0)
    def _():
        acc[...] = jnp.zeros_like(acc)

    a_scr[...] = _dot(u_ref[...], hnT_ref[...]).reshape(a_scr.shape)

    def body(sub, carry):
        i = k * nsub + sub
        for col in range(ncol):
            cs = slice(col * LANES, (col + 1) * LANES)
            w = None
            for hd in range(PEER_HEADS):
                tau = tau_ref[hd, col, pl.ds(i, 1), :]
                e0 = e0_ref[hd, col, pl.ds(i, 1), :]
                t = jnp.where(s1_ref[hd, col] >= tau, e1_ref[hd, col], 0.0) * e0
                w = t if w is None else w + t
            a = a_scr[sub, :, cs]
            act = 0.5 * a * (1.0 + lax.erf(a * sqrt_half))
            p_scr[sub, :, cs] = (w * act).astype(BF16)
        return carry

    lax.fori_loop(0, nsub, body, 0)
    acc[...] += _dot(vt_ref[...], p_scr[...].reshape(nsub * LANES, p_scr.shape[2]))

    @pl.when(k == pl.num_programs(1) - 1)
    def _():
        out_ref[...] = h_ref[...] + acc[...].T


def _peer_dense(h2d, hnT, s1, e1, tau, e0, u_bf, vt_bf, tb, ec):
    n, d = h2d.shape
    ne = u_bf.shape[0]
    ncol = tb // LANES
    tile_spec = pl.BlockSpec((PEER_HEADS, ncol, PEER_KEYS, LANES), lambda t, k: (0, t, 0, 0))
    return pl.pallas_call(
        _peer_dense_kernel,
        grid=(n // tb, ne // ec),
        in_specs=[pl.BlockSpec((d, tb), lambda t, k: (0, t))] + [tile_spec] * 4 + [
            pl.BlockSpec((ec, d), lambda t, k: (k, 0)),
            pl.BlockSpec((d, ec), lambda t, k: (0, k)),
            pl.BlockSpec((tb, d), lambda t, k: (t, 0)),
        ],
        out_specs=pl.BlockSpec((tb, d), lambda t, k: (t, 0)),
        out_shape=jax.ShapeDtypeStruct((n, d), F32),
        scratch_shapes=[pltpu.VMEM((d, tb), F32),
                        pltpu.VMEM((ec // LANES, LANES, tb), F32),
                        pltpu.VMEM((ec // LANES, LANES, tb), BF16)],
        compiler_params=_cparams(2),
        name="peer_dense",
    )(hnT, s1, e1, tau, e0, u_bf, vt_bf, h2d)


def _peer_ffn(h2d, g_ffn, w_q, sub_keys, u_bf, vt_bf, tb, ec):
    hnT, s1, e1, tau, e0 = _peer_pre(h2d, g_ffn, w_q, sub_keys, tb)
    return _peer_dense(h2d, hnT, s1, e1, tau, e0, u_bf, vt_bf, tb, ec)


def _shared_kv_kernel(head_major, h_ref, g_ref, w_ref, gk_ref, bd_ref, k_ref, v_ref, *bf_refs):
    tw = gk_ref.shape[-1]
    xn = _rms(h_ref[0], g_ref[...]).astype(BF16)
    z = _dot(xn, w_ref[...])
    k = _group_rms(z[:, :tw], bd_ref[...], gk_ref[...])
    v = z[:, tw:]
    if not head_major:
        k_ref[0] = k
        v_ref[0] = v
        return
    kt_ref, vh_ref = bf_refs
    k_t = k.T
    k_ref[0] = k_t
    for hd in range(DIFF_HEADS):
        sl = slice(hd * LANES, (hd + 1) * LANES)
        kt_ref[0, hd, 0] = k_t[sl, :].astype(BF16)
        v_ref[0, hd] = v[:, sl]
        vh_ref[0, hd] = v[:, sl].astype(BF16)


def _shared_kv(h, g_kv, w_kv, g_kn, bd, ts, head_major):
    b, s, d = h.shape
    tw = DIFF_HEADS * 2 * GROUP
    const = lambda i, j: (0, 0)
    if head_major:
        out_specs = [pl.BlockSpec((1, tw, ts), lambda i, j: (i, 0, j)),
                     pl.BlockSpec((1, DIFF_HEADS, ts, LANES), lambda i, j: (i, 0, j, 0)),
                     pl.BlockSpec((1, DIFF_HEADS, 1, LANES, ts), lambda i, j: (i, 0, j, 0, 0)),
                     pl.BlockSpec((1, DIFF_HEADS, ts, LANES), lambda i, j: (i, 0, j, 0))]
        out_shape = [jax.ShapeDtypeStruct((b, tw, s), F32),
                     jax.ShapeDtypeStruct((b, DIFF_HEADS, s, LANES), F32),
                     jax.ShapeDtypeStruct((b, DIFF_HEADS, s // ts, LANES, ts), BF16),
                     jax.ShapeDtypeStruct((b, DIFF_HEADS, s, LANES), BF16)]
    else:
        out_specs = [pl.BlockSpec((1, ts, tw), lambda i, j: (i, j, 0))] * 2
        out_shape = [jax.ShapeDtypeStruct((b, s, tw), F32)] * 2
    return pl.pallas_call(
        functools.partial(_shared_kv_kernel, head_major),
        grid=(b, s // ts),
        in_specs=[
            pl.BlockSpec((1, ts, d), lambda i, j: (i, j, 0)),
            pl.BlockSpec((1, d), const),
            pl.BlockSpec(w_kv.shape, const),
            pl.BlockSpec((1, tw), const),
            pl.BlockSpec((256, 256), const),
        ],
        out_specs=out_specs,
        out_shape=out_shape,
        compiler_params=_cparams(2),
        name="shared_kv",
    )(h, g_kv.reshape(1, d), w_kv.astype(BF16), jnp.tile(g_kn, 2 * DIFF_HEADS).reshape(1, tw), bd)


def _rel_bias_kernel(rb_ref, bucket_ref, out_ref):
    hd = pl.program_id(0)
    bk = bucket_ref[0]
    far = rb_ref[N_BUCKETS - 1, hd]
    acc = jnp.zeros(bk.shape, F32)
    for n in range(N_BUCKETS - 1):
        acc = jnp.where(bk == n, rb_ref[n, hd] - far, acc)
    out_ref[0, 0] = jnp.where(bk < 0, NEG, acc)


def _rel_bias_tiles(rel_bias, buckets):
    kinds, r, c = buckets.shape
    return pl.pallas_call(
        _rel_bias_kernel,
        grid=(DIFF_HEADS, kinds),
        in_specs=[pl.BlockSpec(memory_space=pltpu.SMEM),
                  pl.BlockSpec((1, r, c), lambda h, k: (k, 0, 0))],
        out_specs=pl.BlockSpec((1, 1, r, c), lambda h, k: (h, k, 0, 0)),
        out_shape=jax.ShapeDtypeStruct((DIFF_HEADS, kinds, r, c), F32),
        compiler_params=_cparams(2),
        name="rel_bias_tiles",
    )(rel_bias, jnp.asarray(buckets))


def _lambda_value(lam_ref, lam_init):
    lp = lam_ref[...]
    s01 = jnp.sum(lp[0:1] * lp[1:2], axis=-1, keepdims=True)
    s23 = jnp.sum(lp[2:3] * lp[3:4], axis=-1, keepdims=True)
    return jnp.exp(s01) - jnp.exp(s23) + lam_init


def _layer_b_prompt_kernel(lam_init, h_ref, kt_ref, vh_ref, mk_ref, mv_ref, gmix_ref, win_ref, gq_ref,
                           lam_ref, gsub_ref, wout_ref, gmq_ref, bd_ref, hmask_ref, cmask_ref, bias_ref,
                           out_ref, q_scr, o_scr):
    qb = pl.program_id(1)
    t = h_ref.shape[1]
    tw = DIFF_HEADS * LANES
    h = h_ref[0]
    xn = _rms(h, gmix_ref[...]).astype(BF16)
    z = _dot(xn, win_ref[...])
    q = _group_rms(z[:, :tw], bd_ref[...], gq_ref[...]) * (GROUP ** -0.5)
    for hd in range(DIFF_HEADS):
        qh = q[:, hd * LANES:(hd + 1) * LANES]
        for c in range(2):
            q_scr[2 * hd + c] = (qh * cmask_ref[c:c + 1, :]).astype(BF16)
    lam = _lambda_value(lam_ref, lam_init)

    def head_body(hd, carry):
        outs = []
        for c in range(2):
            qc = q_scr[2 * hd + c]

            def kb_body(kb, st):
                m, l, acc = st
                kind = jnp.minimum(qb - kb, 2)
                s = _dot(qc, kt_ref[0, hd, kb]) + bias_ref[hd, kind]
                m_new = jnp.maximum(m, jnp.max(s, axis=-1, keepdims=True))
                alpha = jnp.exp(m - m_new)
                p = jnp.exp(s - m_new)
                l = alpha * l + jnp.sum(p, axis=-1, keepdims=True)
                v = vh_ref[0, hd, pl.ds(pl.multiple_of(kb * t, t), t), :]
                acc = alpha * acc + _dot(p.astype(BF16), v)
                return m_new, l, acc

            init = (jnp.full((t, 1), NEG, F32), jnp.zeros((t, 1), F32), jnp.zeros((t, LANES), F32))
            m, l, acc = lax.fori_loop(0, qb + 1, kb_body, init)
            outs.append(acc * (1.0 / l))
        o = outs[0] - lam * outs[1]
        o_scr[hd] = _rms(o, gsub_ref[...]) * (1.0 - lam_init)
        return carry

    lax.fori_loop(0, DIFF_HEADS, head_body, 0)
    tok = jnp.concatenate([o_scr[hd] for hd in range(DIFF_HEADS)], axis=-1)
    mo = _mem_attn(z[:, tw:], mk_ref[0].astype(BF16), mv_ref[0].astype(BF16), bd_ref[...],
                   gmq_ref[...], hmask_ref[...])
    out_ref[0] = (h + _dot(tok.astype(BF16), wout_ref[0:tw, :])
                  + _dot(mo.astype(BF16), wout_ref[tw:, :]))


def _layer_b_prompt(h, kt, vh, mem_k, mem_v, g_mix, w_in, g_qn, lam_b, g_subln, w_out, g_mqn, bd, hmask,
                    cmask, bias, lam_init, t):
    b, s, d = h.shape
    tw = DIFF_HEADS * LANES
    mw = mem_k.shape[1]
    nm = mem_k.shape[2]
    const = lambda i, j: (0, 0)
    return pl.pallas_call(
        functools.partial(_layer_b_prompt_kernel, lam_init),
        grid=(b, s // t),
        in_specs=[
            pl.BlockSpec((1, t, d), lambda i, j: (i, j, 0)),
            pl.BlockSpec((1,) + kt.shape[1:], lambda i, j: (i, 0, 0, 0, 0)),
            pl.BlockSpec((1,) + vh.shape[1:], lambda i, j: (i, 0, 0, 0)),
            pl.BlockSpec((1, mw, nm), lambda i, j: (i, 0, 0)),
            pl.BlockSpec((1, mw, nm), lambda i, j: (i, 0, 0)),
            pl.BlockSpec((1, d), const),
            pl.BlockSpec(w_in.shape, const),
            pl.BlockSpec((1, tw), const),
            pl.BlockSpec(lam_b.shape, const),
            pl.BlockSpec((1, LANES), const),
            pl.BlockSpec(w_out.shape, const),
            pl.BlockSpec((1, mw), const),
            pl.BlockSpec((256, 256), const),
            pl.BlockSpec(hmask.shape, const),
            pl.BlockSpec(cmask.shape, const),
            pl.BlockSpec(bias.shape, lambda i, j: (0, 0, 0, 0)),
        ],
        out_specs=pl.BlockSpec((1, t, d), lambda i, j: (i, j, 0)),
        out_shape=jax.ShapeDtypeStruct((b, s, d), F32),
        scratch_shapes=[pltpu.VMEM((2 * DIFF_HEADS, t, LANES), BF16),
                        pltpu.VMEM((DIFF_HEADS, t, LANES), F32)],
        compiler_params=_cparams(2),
        name="layer_b_prompt",
    )(h, kt, vh, mem_k, mem_v, g_mix.reshape(1, d), w_in.astype(BF16),
      jnp.tile(g_qn, 2 * DIFF_HEADS).reshape(1, tw), lam_b, g_subln.reshape(1, LANES),
      w_out.astype(BF16), jnp.tile(g_mqn, MEM_HEADS).reshape(1, mw), bd, hmask, cmask, bias)


def _layer_b_sample_kernel(lam_init, pt_ref, h_ref, ck_ref, cv_ref, kn_ref, vn_ref, mk_ref, mv_ref,
                           gmix_ref, win_ref, gq_ref, lam_ref, gsub_ref, wout_ref, gmq_ref, bd_ref,
                           hmask_ref, qmask_ref, bias_ref, out_ref,
                           qbd, qm_scr, m_scr, l_scr, acc, kv_scr):
    del pt_ref
    p = pl.program_id(1)
    last = pl.num_programs(1) - 1
    sq = h_ref.shape[1]
    tw = DIFF_HEADS * LANES
    rows = qbd.shape[0]

    @pl.when(p == 0)
    def _():
        xn = _rms(h_ref[0], gmix_ref[...]).astype(BF16)
        z = _dot(xn, win_ref[...])
        q = _group_rms(z[:, :tw], bd_ref[...], gq_ref[...]) * (GROUP ** -0.5)
        qbd[...] = (jnp.concatenate([q] * (rows // sq), axis=0) * qmask_ref[...]).astype(BF16)
        qm_scr[...] = z[:, tw:]
        m_scr[...] = jnp.full(m_scr.shape, NEG, F32)
        l_scr[...] = jnp.zeros_like(l_scr)
        acc[...] = jnp.zeros_like(acc)

    def update(s, v_heads):
        m = m_scr[...]
        m_new = jnp.maximum(m, jnp.max(s, axis=-1, keepdims=True))
        alpha = jnp.exp(m - m_new)
        pr = jnp.exp(s - m_new)
        l_scr[...] = alpha * l_scr[...] + jnp.sum(pr, axis=-1, keepdims=True)
        pr = pr.astype(BF16)
        pv = [_dot(pr[hd * 2 * sq:(hd + 1) * 2 * sq], v_heads[hd]) for hd in range(DIFF_HEADS)]
        acc[...] = alpha * acc[...] + jnp.concatenate(pv, axis=0)
        m_scr[...] = m_new

    kind = jnp.where(p == last, 1, 0)
    update(_dot(qbd[...], ck_ref[0].astype(BF16)) + bias_ref[kind],
           [cv_ref[0, hd].astype(BF16) for hd in range(DIFF_HEADS)])

    @pl.when(p == last)
    def _():
        kv_scr[...] = jnp.zeros_like(kv_scr)
        kv_scr[0, 0:sq, :] = kn_ref[0]
        kv_scr[1, 0:sq, :] = vn_ref[0]
        v_new = kv_scr[1].astype(BF16)
        update(_dot_nt(qbd[...], kv_scr[0].astype(BF16)) + bias_ref[2],
               [v_new[:, hd * LANES:(hd + 1) * LANES] for hd in range(DIFF_HEADS)])
        lam = _lambda_value(lam_ref, lam_init)
        o_all = acc[...] * (1.0 / l_scr[...])
        toks = []
        for hd in range(DIFF_HEADS):
            r0 = hd * 2 * sq
            o = o_all[r0:r0 + sq] - lam * o_all[r0 + sq:r0 + 2 * sq]
            toks.append(_rms(o, gsub_ref[...]) * (1.0 - lam_init))
        tok = jnp.concatenate(toks, axis=-1)
        mo = _mem_attn(qm_scr[...], mk_ref[0].astype(BF16), mv_ref[0].astype(BF16), bd_ref[...],
                       gmq_ref[...], hmask_ref[...])
        out_ref[0] = (h_ref[0] + _dot(tok.astype(BF16), wout_ref[0:tw, :])
                      + _dot(mo.astype(BF16), wout_ref[tw:, :]))


def _layer_b_sample(h, cache_kt, cache_vh, page_table, k_new, v_new, mem_k, mem_v, g_mix, w_in, g_qn, lam_b,
                    g_subln, w_out, g_mqn, bd, hmask, qmask, bias, lam_init):
    b, sq, d = h.shape
    n_pages = page_table.shape[1]
    page = cache_kt.shape[2]
    tw = DIFF_HEADS * LANES
    mw = mem_k.shape[1]
    nm = mem_k.shape[2]
    rows = 2 * DIFF_HEADS * sq
    const = lambda i, j, pt: (0, 0)
    per_b = lambda i, j, pt: (i, 0, 0)
    grid_spec = pltpu.PrefetchScalarGridSpec(
        num_scalar_prefetch=1,
        grid=(b, n_pages),
        in_specs=[
            pl.BlockSpec((1, sq, d), per_b),
            pl.BlockSpec((1, tw, page), lambda i, j, pt: (pt[i, j], 0, 0)),
            pl.BlockSpec((1, DIFF_HEADS, page, LANES), lambda i, j, pt: (pt[i, j], 0, 0, 0)),
            pl.BlockSpec((1, sq, tw), per_b),
            pl.BlockSpec((1, sq, tw), per_b),
            pl.BlockSpec((1, mw, nm), per_b),
            pl.BlockSpec((1, mw, nm), per_b),
            pl.BlockSpec((1, d), const),
            pl.BlockSpec(w_in.shape, const),
            pl.BlockSpec((1, tw), const),
            pl.BlockSpec(lam_b.shape, const),
            pl.BlockSpec((1, LANES), const),
            pl.BlockSpec(w_out.shape, const),
            pl.BlockSpec((1, mw), const),
            pl.BlockSpec((256, 256), const),
            pl.BlockSpec(hmask.shape, const),
            pl.BlockSpec(qmask.shape, const),
            pl.BlockSpec(bias.shape, lambda i, j, pt: (0, 0, 0)),
        ],
        out_specs=pl.BlockSpec((1, sq, d), per_b),
        scratch_shapes=[pltpu.VMEM((rows, tw), BF16),
                        pltpu.VMEM((sq, mw), F32),
                        pltpu.VMEM((rows, 1), F32),
                        pltpu.VMEM((rows, 1), F32),
                        pltpu.VMEM((rows, LANES), F32),
                        pltpu.VMEM((2, page, tw), F32)],
    )
    return pl.pallas_call(
        functools.partial(_layer_b_sample_kernel, lam_init),
        grid_spec=grid_spec,
        out_shape=jax.ShapeDtypeStruct((b, sq, d), F32),
        compiler_params=_cparams(2),
        name="layer_b_sample",
    )(page_table, h, cache_kt, cache_vh, k_new, v_new, mem_k, mem_v, g_mix.reshape(1, d), w_in.astype(BF16),
      jnp.tile(g_qn, 2 * DIFF_HEADS).reshape(1, tw), lam_b, g_subln.reshape(1, LANES), w_out.astype(BF16),
      jnp.tile(g_mqn, MEM_HEADS).reshape(1, mw), bd, hmask, qmask, bias)


def _prompt_buckets(t):
    r = np.arange(t)[:, None]
    c = np.arange(t)[None, :]
    diag = np.where(c <= r, _bucket_np(r - c), -1)
    sub = _bucket_np(t + r - c)
    far = np.full((t, t), N_BUCKETS - 1, np.int32)
    return np.stack([diag, sub, far]).astype(np.int32)


def _sample_buckets(sq, page):
    qi = np.tile(np.arange(sq), 2)[:, None]
    tok = np.arange(page)[None, :]
    far = np.full((2 * sq, page), N_BUCKETS - 1, np.int32)
    last = _bucket_np(page + qi - tok)
    new = np.where(tok <= qi, _bucket_np(qi - tok), -1)
    return np.stack([far, last, new]).astype(np.int32)


def _sample_qmask(sq):
    rows = np.arange(2 * DIFF_HEADS * sq)[:, None] // sq
    lanes = np.arange(DIFF_HEADS * LANES)[None, :] // GROUP
    return jnp.asarray((rows == lanes).astype(np.float32))


def _block(n, pref):
    return pref if n % pref == 0 else n


def kernel(x_prompt, x_sample, state_conv, cache_k, cache_v, cache_mem_k, cache_mem_v, page_table, mem_prompt, g_mix, g_ffn, w_in_a, w_conv, w_out_a, w_in_b, g_qn_b, lam_b, g_subln, w_out_b, g_kv, w_kv, g_kn, rel_bias, g_mem, w_mem_kv, g_mem_qn, g_mem_kn, w_peer_q, peer_subkeys, peer_u, peer_v):
    depth = g_mix.shape[0]
    n_a = w_in_a.shape[0]
    assert depth == 2 and n_a == 1, "one short-conv layer followed by one differential-attention layer"
    bp, sp, d = x_prompt.shape
    bs, ss, _ = x_sample.shape
    nm = mem_prompt.shape[1]
    mw = MEM_HEADS * GROUP
    tw = DIFF_HEADS * 2 * GROUP
    page = cache_k.shape[1]
    t_attn = _block(sp, 256)

    bd = _block_diag_mean()
    hmask = _mem_head_mask()
    cmask = _sub_head_mask()
    u_bf = peer_u.astype(BF16)
    vt_bf = jnp.swapaxes(peer_v.astype(BF16), 1, 2)

    def mem_t(x):
        return jnp.transpose(x, (0, 1, 3, 4, 2)).reshape(depth, x.shape[1], mw, nm)

    def mem_t_inv(x):
        return jnp.transpose(x.reshape(depth, x.shape[1], MEM_HEADS, GROUP, nm), (0, 1, 4, 2, 3))

    cache_kt = jnp.transpose(cache_k, (0, 2, 3, 4, 1)).reshape(cache_k.shape[0], tw, page)
    cache_vh = jnp.transpose(cache_v, (0, 2, 1, 3))

    mem_k_p, mem_v_p = _mem_kv(mem_prompt, g_mem, w_mem_kv, g_mem_kn, bd)
    mem_k_s = mem_t(cache_mem_k)
    mem_v_s = mem_t(cache_mem_v)

    bias_p = _rel_bias_tiles(rel_bias, _prompt_buckets(t_attn))
    bias_s = jnp.swapaxes(_rel_bias_tiles(rel_bias, _sample_buckets(ss, page)), 0, 1)
    bias_s = bias_s.reshape(3, DIFF_HEADS * 2 * ss, page)
    lam_init = 0.8 - 0.6 * math.exp(-0.3 * 1)

    def ffn(h, l, tb, ec):
        b, s, _ = h.shape
        return _peer_ffn(h.reshape(b * s, d), g_ffn[l], w_peer_q[l], peer_subkeys[l], u_bf[l], vt_bf[l],
                         tb, ec).reshape(b, s, d)

    n = bp * sp
    tb = _block(n, 512)
    zeros_conv = jnp.zeros((bp, 2, w_conv.shape[-1]), F32)
    h, conv_p = _layer_a(x_prompt, zeros_conv, mem_k_p[0], mem_v_p[0], g_mix[0], w_in_a[0], w_conv[0],
                         w_out_a[0], g_mem_qn[0], bd, hmask, ts=_block(sp, 512))
    h = ffn(h, 0, tb, 512)
    k_t, v_h, kt_bf, vh_bf = _shared_kv(h, g_kv, w_kv, g_kn, bd, t_attn, True)
    h = _layer_b_prompt(h, kt_bf, vh_bf, mem_k_p[1], mem_v_p[1], g_mix[1], w_in_b[0], g_qn_b[0], lam_b[0],
                        g_subln[0], w_out_b[0], g_mem_qn[1], bd, hmask, cmask, bias_p, lam_init, t_attn)
    y_p = ffn(h, 1, tb, 512)
    k_p = jnp.transpose(k_t.reshape(bp, DIFF_HEADS, 2, GROUP, sp), (0, 4, 1, 2, 3))
    v_p = jnp.transpose(v_h, (0, 2, 1, 3))

    n = bs * ss
    tb = _block(n, 512)
    h, conv_s = _layer_a(x_sample, state_conv[0], mem_k_s[0], mem_v_s[0], g_mix[0], w_in_a[0], w_conv[0],
                         w_out_a[0], g_mem_qn[0], bd, hmask, ts=_block(ss, 512))
    h = ffn(h, 0, tb, 512)
    k_s, v_s = _shared_kv(h.reshape(1, n, d), g_kv, w_kv, g_kn, bd, n, False)
    k_s = k_s.reshape(bs, ss, tw)
    v_s = v_s.reshape(bs, ss, tw)
    h = _layer_b_sample(h, cache_kt, cache_vh, page_table, k_s, v_s, mem_k_s[1], mem_v_s[1], g_mix[1],
                        w_in_b[0], g_qn_b[0], lam_b[0], g_subln[0], w_out_b[0], g_mem_qn[1], bd, hmask,
                        _sample_qmask(ss), bias_s, lam_init)
    y_s = ffn(h, 1, tb, 512)

    return (y_p, y_s, conv_p[None], conv_s[None], k_p, v_p,
            k_s.reshape(bs, ss, DIFF_HEADS, 2, GROUP), v_s.reshape(bs, ss, DIFF_HEADS, 2 * GROUP),
            mem_t_inv(mem_k_p), mem_t_inv(mem_v_p))
```

```python
import functools
import math

import numpy as np
import jax
import jax.numpy as jnp
from jax import lax
from jax.experimental import pallas as pl
from jax.experimental.pallas import tpu as pltpu

F32 = jnp.float32
BF16 = jnp.bfloat16

EPS = 1e-6
NEG = -1e30
LANES = 128
GROUP = 64
MEM_HEADS = 4
DIFF_HEADS = 6
N_BUCKETS = 32
MAX_DISTANCE = 128
PEER_HEADS = 8
PEER_KEYS = 128
PEER_TOPK = 16
PEER_EXPERT_CHUNK = 1024
VMEM_LIMIT = 56 * 1024 * 1024


def _cparams(n_grid):
    return pltpu.CompilerParams(dimension_semantics=("arbitrary",) * n_grid,
                                vmem_limit_bytes=VMEM_LIMIT)


def _dot(a, b):
    return jnp.dot(a, b, preferred_element_type=F32)


def _dot_nt(a, b):
    return lax.dot_general(a, b, (((1,), (1,)), ((), ())), preferred_element_type=F32)


def _rms(x, g):
    ms = jnp.mean(x * x, axis=-1, keepdims=True)
    return x * lax.rsqrt(ms + EPS) * g


def _group_rms(x, bd, g):
    x2 = x * x
    hi = x2.astype(BF16)
    lo = (x2 - hi.astype(F32)).astype(BF16)
    parts = []
    for s in range(x.shape[-1] // 256):
        sl = slice(s * 256, (s + 1) * 256)
        parts.append(_dot(hi[:, sl], bd) + _dot(lo[:, sl], bd))
    ms = parts[0] if len(parts) == 1 else jnp.concatenate(parts, axis=-1)
    return x * lax.rsqrt(ms + EPS) * g


def _mem_attn(qm, mk_t, mv_t, bd, gq, hmask):
    q = _group_rms(qm, bd, gq) * (GROUP ** -0.5)
    out = None
    for hd in range(MEM_HEADS):
        msk = hmask[hd:hd + 1, :]
        s = _dot((q * msk).astype(BF16), mk_t)
        m = jnp.max(s, axis=-1, keepdims=True)
        p = jnp.exp(s - m)
        l = jnp.sum(p, axis=-1, keepdims=True)
        t = _dot_nt(p.astype(BF16), mv_t) * (1.0 / l) * msk
        out = t if out is None else out + t
    return out


def _block_diag_mean():
    i = np.arange(256)
    return jnp.asarray((i[:, None] // GROUP == i[None, :] // GROUP).astype(np.float32) / GROUP, BF16)


def _mem_head_mask():
    i = np.arange(MEM_HEADS * GROUP)
    return jnp.asarray((i[None, :] // GROUP == np.arange(MEM_HEADS)[:, None]).astype(np.float32))


def _sub_head_mask():
    i = np.arange(2 * GROUP)
    return jnp.asarray((i[None, :] // GROUP == np.arange(2)[:, None]).astype(np.float32))


def _bucket_np(n):
    n = np.maximum(n, 0)
    max_exact = N_BUCKETS // 2
    ratio = np.log(np.maximum(n, 1).astype(np.float32) / np.float32(max_exact)) / np.float32(
        math.log(MAX_DISTANCE / max_exact))
    large = max_exact + (ratio * np.float32(N_BUCKETS - max_exact)).astype(np.int32)
    large = np.minimum(large, N_BUCKETS - 1)
    return np.where(n < max_exact, n, large).astype(np.int32)


def _mem_kv_kernel(mem_ref, g_ref, w_ref, gk_ref, bd_ref, k_ref, v_ref):
    xn = _rms(mem_ref[0], g_ref[0]).astype(BF16)
    z = _dot(xn, w_ref[0])
    w = z.shape[-1] // 2
    k_ref[0, 0] = _group_rms(z[:, :w], bd_ref[...], gk_ref[0]).T
    v_ref[0, 0] = z[:, w:].T


def _mem_kv(mem, g_mem, w_mem_kv, g_kn, bd):
    b, nm, d = mem.shape
    depth = g_mem.shape[0]
    w = w_mem_kv.shape[-1] // 2
    out = jax.ShapeDtypeStruct((depth, b, w, nm), F32)
    return pl.pallas_call(
        _mem_kv_kernel,
        grid=(depth, b),
        in_specs=[
            pl.BlockSpec((1, nm, d), lambda l, i: (i, 0, 0)),
            pl.BlockSpec((1, 1, d), lambda l, i: (l, 0, 0)),
            pl.BlockSpec((1, d, 2 * w), lambda l, i: (l, 0, 0)),
            pl.BlockSpec((1, 1, w), lambda l, i: (l, 0, 0)),
            pl.BlockSpec((256, 256), lambda l, i: (0, 0)),
        ],
        out_specs=[pl.BlockSpec((1, 1, w, nm), lambda l, i: (l, i, 0, 0))] * 2,
        out_shape=[out, out],
        compiler_params=_cparams(2),
        name="mem_kv",
    )(mem, g_mem.reshape(depth, 1, d), w_mem_kv.astype(BF16),
      jnp.tile(g_kn, (1, MEM_HEADS)).reshape(depth, 1, w), bd)


def _layer_a_kernel(h_ref, prev_ref, mk_ref, mv_ref, gmix_ref, win_ref, wconv_ref, wout_ref, gq_ref,
                    bd_ref, hmask_ref, out_ref, conv_ref, ubuf):
    j = pl.program_id(1)
    ts = h_ref.shape[1]
    tw = wconv_ref.shape[1]
    h = h_ref[0]
    xn = _rms(h, gmix_ref[...]).astype(BF16)
    z = _dot(xn, win_ref[...])
    u = z[:, tw:2 * tw] * z[:, 2 * tw:3 * tw]

    @pl.when(j == 0)
    def _():
        ubuf[6:8, :] = prev_ref[0]

    @pl.when(j > 0)
    def _():
        ubuf[6:8, :] = ubuf[ts + 6:ts + 8, :]

    ubuf[8:8 + ts, :] = u
    conv = (wconv_ref[0:1, :] * ubuf[6:6 + ts, :] + wconv_ref[1:2, :] * ubuf[7:7 + ts, :]
            + wconv_ref[2:3, :] * u)
    conv_ref[0] = ubuf[ts + 6:ts + 8, :]
    tok = z[:, :tw] * conv
    mo = _mem_attn(z[:, 3 * tw:], mk_ref[0].astype(BF16), mv_ref[0].astype(BF16), bd_ref[...],
                   gq_ref[...], hmask_ref[...])
    out_ref[0] = (h + _dot(tok.astype(BF16), wout_ref[0:tw, :])
                  + _dot(mo.astype(BF16), wout_ref[tw:, :]))


def _layer_a(h, conv_prev, mem_k, mem_v, g_mix, w_in, w_conv, w_out, g_qn, bd, hmask, ts):
    b, s, d = h.shape
    tw = w_conv.shape[1]
    mw = mem_k.shape[1]
    nm = mem_k.shape[2]
    const = lambda i, j: (0, 0)
    return pl.pallas_call(
        _layer_a_kernel,
        grid=(b, s // ts),
        in_specs=[
            pl.BlockSpec((1, ts, d), lambda i, j: (i, j, 0)),
            pl.BlockSpec((1, 2, tw), lambda i, j: (i, 0, 0)),
            pl.BlockSpec((1, mw, nm), lambda i, j: (i, 0, 0)),
            pl.BlockSpec((1, mw, nm), lambda i, j: (i, 0, 0)),
            pl.BlockSpec((1, d), const),
            pl.BlockSpec(w_in.shape, const),
            pl.BlockSpec(w_conv.shape, const),
            pl.BlockSpec(w_out.shape, const),
            pl.BlockSpec((1, mw), const),
            pl.BlockSpec((256, 256), const),
            pl.BlockSpec(hmask.shape, const),
        ],
        out_specs=[pl.BlockSpec((1, ts, d), lambda i, j: (i, j, 0)),
                   pl.BlockSpec((1, 2, tw), lambda i, j: (i, 0, 0))],
        out_shape=[jax.ShapeDtypeStruct((b, s, d), F32), jax.ShapeDtypeStruct((b, 2, tw), F32)],
        scratch_shapes=[pltpu.VMEM((ts + 8, tw), F32)],
        compiler_params=_cparams(2),
        name="layer_a",
    )(h, conv_prev, mem_k, mem_v, g_mix.reshape(1, d), w_in.astype(BF16), w_conv, w_out.astype(BF16),
      jnp.tile(g_qn, MEM_HEADS).reshape(1, mw), bd, hmask)


def _top16_rows(v):
    rows = []
    for r in range(PEER_TOPK):
        m = jnp.max(v, axis=0, keepdims=True)
        rows.append(m)
        if r + 1 < PEER_TOPK:
            v = jnp.where(v == m, -jnp.inf, v)
    return jnp.concatenate(rows, axis=0)


def _peer_pre_kernel(h_ref, g_ref, wq_ref, sk_ref, hnT_ref, k1_ref, e1_ref, kap_ref, e0_ref, sc_scr):
    tb = h_ref.shape[0]
    ncol = tb // LANES
    hn = _rms(h_ref[...], g_ref[...])
    hnT_ref[...] = hn.T.astype(BF16)
    q = _dot(hn.astype(BF16), wq_ref[...]).astype(BF16)
    for hc in range(2 * PEER_HEADS):
        sc_t = _dot_nt(sk_ref[hc % 2], q[:, hc * LANES:(hc + 1) * LANES])
        for col in range(ncol):
            sc_scr[hc, col] = sc_t[:, col * LANES:(col + 1) * LANES]

    def body(idx, carry):
        hd = idx // ncol
        col = idx % ncol
        s0 = sc_scr[2 * hd, col]
        s1 = sc_scr[2 * hd + 1, col]
        a = _top16_rows(s0)
        b = _top16_rows(s1)
        cand = jnp.concatenate(
            [a[0:1] + b] + [a[r:r + 1] + b[0:8] for r in range(1, 8)] + [a[8:16] + b[0:1]], axis=0)
        v = cand
        for _ in range(PEER_TOPK - 1):
            m = jnp.max(v, axis=0, keepdims=True)
            v = jnp.where(v == m, -jnp.inf, v)
        t = jnp.max(v, axis=0, keepdims=True)
        top = a[0:1] + b[0:1]
        zsum = jnp.sum(jnp.where(cand >= t, jnp.exp(cand - top), 0.0), axis=0, keepdims=True)
        nq = jnp.zeros(s0.shape, F32)
        k1 = jnp.zeros(s1.shape, F32)
        for c in range(PEER_TOPK):
            bc = b[c:c + 1]
            nq = nq + jnp.where(s0 + bc >= t, 1.0, 0.0)
            k1 = k1 + jnp.where(s1 >= bc, 1.0, 0.0)
        kap = jnp.where(s0 >= a[PEER_TOPK - 1:PEER_TOPK], (PEER_TOPK + 1.0) - nq, PEER_TOPK + 1.0)
        k1_ref[hd, col] = pltpu.bitcast(k1.astype(BF16), jnp.uint32)
        e1_ref[hd, col] = pltpu.bitcast((jnp.exp(s1 - b[0:1]) * (1.0 / zsum)).astype(BF16), jnp.uint32)
        kap_ref[hd, col] = kap
        e0_ref[hd, col] = jnp.exp(s0 - a[0:1])
        return carry

    lax.fori_loop(0, PEER_HEADS * ncol, body, 0, unroll=2)


def _packed_rows():
    return PEER_KEYS * jnp.dtype(BF16).itemsize // 4


def _peer_pre(h2d, g_ffn, w_q, sub_keys, tb):
    n, d = h2d.shape
    ncols = n // LANES
    ncol = tb // LANES
    const2 = lambda i: (0, 0)
    prow = _packed_rows()
    tiles = ([jax.ShapeDtypeStruct((PEER_HEADS, ncols, prow, LANES), jnp.uint32)] * 2
             + [jax.ShapeDtypeStruct((PEER_HEADS, ncols, PEER_KEYS, LANES), F32)] * 2)
    packed_spec = pl.BlockSpec((PEER_HEADS, ncol, prow, LANES), lambda i: (0, i, 0, 0))
    tile_spec = pl.BlockSpec((PEER_HEADS, ncol, PEER_KEYS, LANES), lambda i: (0, i, 0, 0))
    return pl.pallas_call(
        _peer_pre_kernel,
        grid=(n // tb,),
        in_specs=[
            pl.BlockSpec((tb, d), lambda i: (i, 0)),
            pl.BlockSpec((1, d), const2),
            pl.BlockSpec(w_q.shape, const2),
            pl.BlockSpec(sub_keys.shape, lambda i: (0, 0, 0)),
        ],
        out_specs=[pl.BlockSpec((d, tb), lambda i: (0, i))] + [packed_spec] * 2 + [tile_spec] * 2,
        out_shape=[jax.ShapeDtypeStruct((d, n), BF16)] + tiles,
        scratch_shapes=[pltpu.VMEM((2 * PEER_HEADS, ncol, PEER_KEYS, LANES), F32)],
        compiler_params=_cparams(1),
        name="peer_pre",
    )(h2d, g_ffn.reshape(1, d), w_q.astype(BF16), sub_keys.astype(BF16))


def _row_tile_bf16(ref, hd, col, i):
    row = jnp.broadcast_to(ref[hd, col, pl.ds(i, 1), :], (16, LANES)).astype(BF16)
    return jnp.concatenate([row] * (PEER_KEYS // 16), axis=0)


def _gate_slab(a_ref, p_ref, sub, i, k1_ref, e1_ref, kap_ref, e0_ref):
    sqrt_half = np.float32(np.sqrt(0.5))
    for col in range(a_ref.shape[2] // LANES):
        cs = slice(col * LANES, (col + 1) * LANES)
        w = None
        for hd in range(PEER_HEADS):
            kap = _row_tile_bf16(kap_ref, hd, col, i)
            e0 = _row_tile_bf16(e0_ref, hd, col, i)
            k1 = pltpu.bitcast(k1_ref[hd, col], BF16)
            e1 = pltpu.bitcast(e1_ref[hd, col], BF16)
            t = jnp.where(k1 >= kap, e1, jnp.zeros((), BF16)) * e0
            w = t if w is None else w + t
        a = a_ref[sub, :, cs]
        act = 0.5 * a * (1.0 + lax.erf(a * sqrt_half))
        p_ref[sub, :, cs] = w * act.astype(BF16)


def _peer_dense_kernel(hnT_ref, k1_ref, e1_ref, kap_ref, e0_ref, u_ref, vt_ref, h_ref, out_ref, acc, a_scr, p_scr):
    k = pl.program_id(1)
    nsub = a_scr.shape[0]

    @pl.when(k == 0)
    def _():
        acc[...] = jnp.zeros_like(acc)

    a_scr[...] = _dot(u_ref[...], hnT_ref[...]).reshape(a_scr.shape)

    def body(sub, carry):
        _gate_slab(a_scr, p_scr, sub, k * nsub + sub, k1_ref, e1_ref, kap_ref, e0_ref)
        return carry

    lax.fori_loop(0, nsub, body, 0)
    acc[...] += _dot(vt_ref[...], p_scr[...].reshape(nsub * LANES, p_scr.shape[2]))

    @pl.when(k == pl.num_programs(1) - 1)
    def _():
        out_ref[...] = h_ref[...] + acc[...].T


def _peer_dense(h2d, hnT, k1, e1, kap, e0, u_bf, vt_bf, tb, ec):
    n, d = h2d.shape
    ne = u_bf.shape[0]
    ncol = tb // LANES
    packed_spec = pl.BlockSpec((PEER_HEADS, ncol, _packed_rows(), LANES), lambda t, k: (0, t, 0, 0))
    tile_spec = pl.BlockSpec((PEER_HEADS, ncol, PEER_KEYS, LANES), lambda t, k: (0, t, 0, 0))
    return pl.pallas_call(
        _peer_dense_kernel,
        grid=(n // tb, ne // ec),
        in_specs=[pl.BlockSpec((d, tb), lambda t, k: (0, t))] + [packed_spec] * 2 + [tile_spec] * 2 + [
            pl.BlockSpec((ec, d), lambda t, k: (k, 0)),
            pl.BlockSpec((d, ec), lambda t, k: (0, k)),
            pl.BlockSpec((tb, d), lambda t, k: (t, 0)),
        ],
        out_specs=pl.BlockSpec((tb, d), lambda t, k: (t, 0)),
        out_shape=jax.ShapeDtypeStruct((n, d), F32),
        scratch_shapes=[pltpu.VMEM((d, tb), F32),
                        pltpu.VMEM((ec // LANES, LANES, tb), F32),
                        pltpu.VMEM((ec // LANES, LANES, tb), BF16)],
        compiler_params=_cparams(2),
        name="peer_dense",
    )(hnT, k1, e1, kap, e0, u_bf, vt_bf, h2d)


def _peer_ffn(h2d, g_ffn, w_q, sub_keys, u_bf, vt_bf, tb, ec):
    hnT, k1, e1, kap, e0 = _peer_pre(h2d, g_ffn, w_q, sub_keys, tb)
    return _peer_dense(h2d, hnT, k1, e1, kap, e0, u_bf, vt_bf, tb, ec)


def _shared_kv_kernel(head_major, h_ref, g_ref, w_ref, gk_ref, bd_ref, k_ref, v_ref, *bf_refs):
    tw = gk_ref.shape[-1]
    xn = _rms(h_ref[0], g_ref[...]).astype(BF16)
    z = _dot(xn, w_ref[...])
    k = _group_rms(z[:, :tw], bd_ref[...], gk_ref[...])
    v = z[:, tw:]
    if not head_major:
        k_ref[0] = k
        v_ref[0] = v
        return
    kt_ref, vh_ref = bf_refs
    k_t = k.T
    k_ref[0] = k_t
    for hd in range(DIFF_HEADS):
        sl = slice(hd * LANES, (hd + 1) * LANES)
        kt_ref[0, hd] = k_t[sl, :].astype(BF16)
        v_ref[0, hd] = v[:, sl]
        vh_ref[0, hd] = v[:, sl].astype(BF16)


def _shared_kv(h, g_kv, w_kv, g_kn, bd, ts, head_major):
    b, s, d = h.shape
    tw = DIFF_HEADS * 2 * GROUP
    const = lambda i, j: (0, 0)
    if head_major:
        out_specs = [pl.BlockSpec((1, tw, ts), lambda i, j: (i, 0, j)),
                     pl.BlockSpec((1, DIFF_HEADS, ts, LANES), lambda i, j: (i, 0, j, 0)),
                     pl.BlockSpec((1, DIFF_HEADS, LANES, ts), lambda i, j: (i, 0, 0, j)),
                     pl.BlockSpec((1, DIFF_HEADS, ts, LANES), lambda i, j: (i, 0, j, 0))]
        out_shape = [jax.ShapeDtypeStruct((b, tw, s), F32),
                     jax.ShapeDtypeStruct((b, DIFF_HEADS, s, LANES), F32),
                     jax.ShapeDtypeStruct((b, DIFF_HEADS, LANES, s), BF16),
                     jax.ShapeDtypeStruct((b, DIFF_HEADS, s, LANES), BF16)]
    else:
        out_specs = [pl.BlockSpec((1, ts, tw), lambda i, j: (i, j, 0))] * 2
        out_shape = [jax.ShapeDtypeStruct((b, s, tw), F32)] * 2
    return pl.pallas_call(
        functools.partial(_shared_kv_kernel, head_major),
        grid=(b, s // ts),
        in_specs=[
            pl.BlockSpec((1, ts, d), lambda i, j: (i, j, 0)),
            pl.BlockSpec((1, d), const),
            pl.BlockSpec(w_kv.shape, const),
            pl.BlockSpec((1, tw), const),
            pl.BlockSpec((256, 256), const),
        ],
        out_specs=out_specs,
        out_shape=out_shape,
        compiler_params=_cparams(2),
        name="shared_kv",
    )(h, g_kv.reshape(1, d), w_kv.astype(BF16), jnp.tile(g_kn, 2 * DIFF_HEADS).reshape(1, tw), bd)


def _rel_bias_kernel(rb_ref, bucket_ref, out_ref):
    hd = pl.program_id(0)
    bk = bucket_ref[0]
    far = rb_ref[N_BUCKETS - 1, hd]
    acc = jnp.zeros(bk.shape, F32)
    for n in range(N_BUCKETS - 1):
        acc = jnp.where(bk == n, rb_ref[n, hd] - far, acc)
    out_ref[0, 0] = jnp.where(bk < 0, NEG, acc)


def _rel_bias_tiles(rel_bias, buckets):
    kinds, r, c = buckets.shape
    return pl.pallas_call(
        _rel_bias_kernel,
        grid=(DIFF_HEADS, kinds),
        in_specs=[pl.BlockSpec(memory_space=pltpu.SMEM),
                  pl.BlockSpec((1, r, c), lambda h, k: (k, 0, 0))],
        out_specs=pl.BlockSpec((1, 1, r, c), lambda h, k: (h, k, 0, 0)),
        out_shape=jax.ShapeDtypeStruct((DIFF_HEADS, kinds, r, c), F32),
        compiler_params=_cparams(2),
        name="rel_bias_tiles",
    )(rel_bias, jnp.asarray(buckets))


def _lambda_value(lam_ref, lam_init):
    lp = lam_ref[...]
    s01 = jnp.sum(lp[0:1] * lp[1:2], axis=-1, keepdims=True)
    s23 = jnp.sum(lp[2:3] * lp[3:4], axis=-1, keepdims=True)
    return jnp.exp(s01) - jnp.exp(s23) + lam_init


def _layer_b_prompt_kernel(lam_init, h_ref, kt_ref, vh_ref, mk_ref, mv_ref, gmix_ref, win_ref, gq_ref,
                           lam_ref, gsub_ref, wout_ref, gmq_ref, bd_ref, hmask_ref, cmask_ref, bias_ref,
                           out_ref, q_scr, o_scr):
    qb = pl.program_id(1)
    t = h_ref.shape[1]
    tw = DIFF_HEADS * LANES
    h = h_ref[0]
    xn = _rms(h, gmix_ref[...]).astype(BF16)
    z = _dot(xn, win_ref[...])
    q = _group_rms(z[:, :tw], bd_ref[...], gq_ref[...]) * (GROUP ** -0.5)
    for hd in range(DIFF_HEADS):
        qh = q[:, hd * LANES:(hd + 1) * LANES]
        for c in range(2):
            q_scr[2 * hd + c] = (qh * cmask_ref[c:c + 1, :]).astype(BF16)
    lam = _lambda_value(lam_ref, lam_init)

    def attend(nk):
        w = nk * t

        def head_body(hd, carry):
            outs = []
            for c in range(2):
                s = _dot(q_scr[2 * hd + c], kt_ref[0, hd, :, 0:w])
                near = [s[:, w - t:] + bias_ref[hd, 0]]
                if nk > 1:
                    near = [s[:, w - 2 * t:w - t] + bias_ref[hd, 1]] + near
                if nk > 2:
                    near = [s[:, :w - 2 * t]] + near
                s = near[0] if len(near) == 1 else jnp.concatenate(near, axis=-1)
                p = jnp.exp(s - jnp.max(s, axis=-1, keepdims=True))
                l = jnp.sum(p, axis=-1, keepdims=True)
                outs.append(_dot(p.astype(BF16), vh_ref[0, hd, 0:w, :]) * (1.0 / l))
            o = outs[0] - lam * outs[1]
            o_scr[hd] = _rms(o, gsub_ref[...]) * (1.0 - lam_init)
            return carry

        lax.fori_loop(0, DIFF_HEADS, head_body, 0)

    for nk in range(1, kt_ref.shape[3] // t + 1):
        pl.when(qb == nk - 1)(functools.partial(attend, nk))

    tok = jnp.concatenate([o_scr[hd] for hd in range(DIFF_HEADS)], axis=-1)
    mo = _mem_attn(z[:, tw:], mk_ref[0].astype(BF16), mv_ref[0].astype(BF16), bd_ref[...],
                   gmq_ref[...], hmask_ref[...])
    out_ref[0] = (h + _dot(tok.astype(BF16), wout_ref[0:tw, :])
                  + _dot(mo.astype(BF16), wout_ref[tw:, :]))


def _layer_b_prompt(h, kt, vh, mem_k, mem_v, g_mix, w_in, g_qn, lam_b, g_subln, w_out, g_mqn, bd, hmask,
                    cmask, bias, lam_init, t):
    b, s, d = h.shape
    tw = DIFF_HEADS * LANES
    mw = mem_k.shape[1]
    nm = mem_k.shape[2]
    const = lambda i, j: (0, 0)
    return pl.pallas_call(
        functools.partial(_layer_b_prompt_kernel, lam_init),
        grid=(b, s // t),
        in_specs=[
            pl.BlockSpec((1, t, d), lambda i, j: (i, j, 0)),
            pl.BlockSpec((1,) + kt.shape[1:], lambda i, j: (i, 0, 0, 0)),
            pl.BlockSpec((1,) + vh.shape[1:], lambda i, j: (i, 0, 0, 0)),
            pl.BlockSpec((1, mw, nm), lambda i, j: (i, 0, 0)),
            pl.BlockSpec((1, mw, nm), lambda i, j: (i, 0, 0)),
            pl.BlockSpec((1, d), const),
            pl.BlockSpec(w_in.shape, const),
            pl.BlockSpec((1, tw), const),
            pl.BlockSpec(lam_b.shape, const),
            pl.BlockSpec((1, LANES), const),
            pl.BlockSpec(w_out.shape, const),
            pl.BlockSpec((1, mw), const),
            pl.BlockSpec((256, 256), const),
            pl.BlockSpec(hmask.shape, const),
            pl.BlockSpec(cmask.shape, const),
            pl.BlockSpec(bias.shape, lambda i, j: (0, 0, 0, 0)),
        ],
        out_specs=pl.BlockSpec((1, t, d), lambda i, j: (i, j, 0)),
        out_shape=jax.ShapeDtypeStruct((b, s, d), F32),
        scratch_shapes=[pltpu.VMEM((2 * DIFF_HEADS, t, LANES), BF16),
                        pltpu.VMEM((DIFF_HEADS, t, LANES), F32)],
        compiler_params=_cparams(2),
        name="layer_b_prompt",
    )(h, kt, vh, mem_k, mem_v, g_mix.reshape(1, d), w_in.astype(BF16),
      jnp.tile(g_qn, 2 * DIFF_HEADS).reshape(1, tw), lam_b, g_subln.reshape(1, LANES),
      w_out.astype(BF16), jnp.tile(g_mqn, MEM_HEADS).reshape(1, mw), bd, hmask, cmask, bias)


def _layer_b_sample_kernel(lam_init, npg, pt_ref, h_ref, *refs):
    del pt_ref
    ck_refs, cv_refs = refs[:npg], refs[npg:2 * npg]
    (kn_ref, vn_ref, mk_ref, mv_ref, gmix_ref, win_ref, gq_ref, lam_ref, gsub_ref, wout_ref, gmq_ref, bd_ref,
     hmask_ref, qmask_ref, bias_ref, bias_new_ref, out_ref, qbd, qm_scr, m_scr, l_scr, acc, kv_scr) = refs[2 * npg:]
    p = pl.program_id(1)
    last = pl.num_programs(1) - 1
    sq = h_ref.shape[1]
    tw = DIFF_HEADS * LANES
    rows = qbd.shape[0]

    @pl.when(p == 0)
    def _():
        xn = _rms(h_ref[0], gmix_ref[...]).astype(BF16)
        z = _dot(xn, win_ref[...])
        q = _group_rms(z[:, :tw], bd_ref[...], gq_ref[...]) * (GROUP ** -0.5)
        qbd[...] = (jnp.concatenate([q] * (rows // sq), axis=0) * qmask_ref[...]).astype(BF16)
        qm_scr[...] = z[:, tw:]
        m_scr[...] = jnp.full(m_scr.shape, NEG, F32)
        l_scr[...] = jnp.zeros_like(l_scr)
        acc[...] = jnp.zeros_like(acc)

    def update(s, v_heads):
        m = m_scr[...]
        m_new = jnp.maximum(m, jnp.max(s, axis=-1, keepdims=True))
        alpha = jnp.exp(m - m_new)
        pr = jnp.exp(s - m_new)
        l_scr[...] = alpha * l_scr[...] + jnp.sum(pr, axis=-1, keepdims=True)
        pr = pr.astype(BF16)
        pv = [_dot(pr[hd * 2 * sq:(hd + 1) * 2 * sq], v_heads[hd]) for hd in range(DIFF_HEADS)]
        acc[...] = alpha * acc[...] + jnp.concatenate(pv, axis=0)
        m_scr[...] = m_new

    kind = jnp.where(p == last, 1, 0)
    kt = jnp.concatenate([ck_refs[t][0].astype(BF16) for t in range(npg)], axis=1)
    update(_dot(qbd[...], kt) + bias_ref[kind],
           [jnp.concatenate([cv_refs[t][0, hd].astype(BF16) for t in range(npg)], axis=0)
            for hd in range(DIFF_HEADS)])

    @pl.when(p == last)
    def _():
        kv_scr[...] = jnp.zeros_like(kv_scr)
        kv_scr[0, 0:sq, :] = kn_ref[0]
        kv_scr[1, 0:sq, :] = vn_ref[0]
        v_new = kv_scr[1].astype(BF16)
        update(_dot_nt(qbd[...], kv_scr[0].astype(BF16)) + bias_new_ref[...],
               [v_new[:, hd * LANES:(hd + 1) * LANES] for hd in range(DIFF_HEADS)])
        lam = _lambda_value(lam_ref, lam_init)
        o_all = acc[...] * (1.0 / l_scr[...])
        toks = []
        for hd in range(DIFF_HEADS):
            r0 = hd * 2 * sq
            o = o_all[r0:r0 + sq] - lam * o_all[r0 + sq:r0 + 2 * sq]
            toks.append(_rms(o, gsub_ref[...]) * (1.0 - lam_init))
        tok = jnp.concatenate(toks, axis=-1)
        mo = _mem_attn(qm_scr[...], mk_ref[0].astype(BF16), mv_ref[0].astype(BF16), bd_ref[...],
                       gmq_ref[...], hmask_ref[...])
        out_ref[0] = (h_ref[0] + _dot(tok.astype(BF16), wout_ref[0:tw, :])
                      + _dot(mo.astype(BF16), wout_ref[tw:, :]))


def _layer_b_sample(h, cache_kt, cache_vh, page_table, k_new, v_new, mem_k, mem_v, g_mix, w_in, g_qn, lam_b,
                    g_subln, w_out, g_mqn, bd, hmask, qmask, bias, bias_new, lam_init, npg):
    b, sq, d = h.shape
    n_pages = page_table.shape[1]
    page = cache_kt.shape[2]
    tw = DIFF_HEADS * LANES
    mw = mem_k.shape[1]
    nm = mem_k.shape[2]
    rows = 2 * DIFF_HEADS * sq
    const = lambda i, j, pt: (0, 0)
    per_b = lambda i, j, pt: (i, 0, 0)

    def page_map(t, nd):
        return lambda i, j, pt: (pt[i, j * npg + t],) + (0,) * nd

    grid_spec = pltpu.PrefetchScalarGridSpec(
        num_scalar_prefetch=1,
        grid=(b, n_pages // npg),
        in_specs=[pl.BlockSpec((1, sq, d), per_b)]
        + [pl.BlockSpec((1, tw, page), page_map(t, 2)) for t in range(npg)]
        + [pl.BlockSpec((1, DIFF_HEADS, page, LANES), page_map(t, 3)) for t in range(npg)]
        + [
            pl.BlockSpec((1, sq, tw), per_b),
            pl.BlockSpec((1, sq, tw), per_b),
            pl.BlockSpec((1, mw, nm), per_b),
            pl.BlockSpec((1, mw, nm), per_b),
            pl.BlockSpec((1, d), const),
            pl.BlockSpec(w_in.shape, const),
            pl.BlockSpec((1, tw), const),
            pl.BlockSpec(lam_b.shape, const),
            pl.BlockSpec((1, LANES), const),
            pl.BlockSpec(w_out.shape, const),
            pl.BlockSpec((1, mw), const),
            pl.BlockSpec((256, 256), const),
            pl.BlockSpec(hmask.shape, const),
            pl.BlockSpec(qmask.shape, const),
            pl.BlockSpec(bias.shape, lambda i, j, pt: (0, 0, 0)),
            pl.BlockSpec(bias_new.shape, const),
        ],
        out_specs=pl.BlockSpec((1, sq, d), per_b),
        scratch_shapes=[pltpu.VMEM((rows, tw), BF16),
                        pltpu.VMEM((sq, mw), F32),
                        pltpu.VMEM((rows, 1), F32),
                        pltpu.VMEM((rows, 1), F32),
                        pltpu.VMEM((rows, LANES), F32),
                        pltpu.VMEM((2, page, tw), F32)],
    )
    return pl.pallas_call(
        functools.partial(_layer_b_sample_kernel, lam_init, npg),
        grid_spec=grid_spec,
        out_shape=jax.ShapeDtypeStruct((b, sq, d), F32),
        compiler_params=_cparams(2),
        name="layer_b_sample",
    )(page_table, h, *([cache_kt] * npg), *([cache_vh] * npg), k_new, v_new, mem_k, mem_v,
      g_mix.reshape(1, d), w_in.astype(BF16), jnp.tile(g_qn, 2 * DIFF_HEADS).reshape(1, tw), lam_b,
      g_subln.reshape(1, LANES), w_out.astype(BF16), jnp.tile(g_mqn, MEM_HEADS).reshape(1, mw), bd, hmask,
      qmask, bias, bias_new)


def _prompt_buckets(t):
    r = np.arange(t)[:, None]
    c = np.arange(t)[None, :]
    diag = np.where(c <= r, _bucket_np(r - c), -1)
    sub = _bucket_np(t + r - c)
    return np.stack([diag, sub]).astype(np.int32)


def _sample_buckets(sq, page):
    qi = np.tile(np.arange(sq), 2)[:, None]
    tok = np.arange(page)[None, :]
    far = np.full((2 * sq, page), N_BUCKETS - 1, np.int32)
    last = _bucket_np(page + qi - tok)
    new = np.where(tok <= qi, _bucket_np(qi - tok), -1)
    return np.stack([far, last, new]).astype(np.int32)


def _sample_qmask(sq):
    rows = np.arange(2 * DIFF_HEADS * sq)[:, None] // sq
    lanes = np.arange(DIFF_HEADS * LANES)[None, :] // GROUP
    return jnp.asarray((rows == lanes).astype(np.float32))


def _block(n, pref):
    return pref if n % pref == 0 else n


def kernel(x_prompt, x_sample, state_conv, cache_k, cache_v, cache_mem_k, cache_mem_v, page_table, mem_prompt, g_mix, g_ffn, w_in_a, w_conv, w_out_a, w_in_b, g_qn_b, lam_b, g_subln, w_out_b, g_kv, w_kv, g_kn, rel_bias, g_mem, w_mem_kv, g_mem_qn, g_mem_kn, w_peer_q, peer_subkeys, peer_u, peer_v):
    depth = g_mix.shape[0]
    n_a = w_in_a.shape[0]
    assert depth == 2 and n_a == 1, "one short-conv layer followed by one differential-attention layer"
    bp, sp, d = x_prompt.shape
    bs, ss, _ = x_sample.shape
    nm = mem_prompt.shape[1]
    mw = MEM_HEADS * GROUP
    tw = DIFF_HEADS * 2 * GROUP
    page = cache_k.shape[1]
    t_attn = _block(sp, 256)

    bd = _block_diag_mean()
    hmask = _mem_head_mask()
    cmask = _sub_head_mask()
    u_bf = peer_u.astype(BF16)
    vt_bf = jnp.swapaxes(peer_v.astype(BF16), 1, 2)

    def mem_t(x):
        return jnp.transpose(x, (0, 1, 3, 4, 2)).reshape(depth, x.shape[1], mw, nm)

    def mem_t_inv(x):
        return jnp.transpose(x.reshape(depth, x.shape[1], MEM_HEADS, GROUP, nm), (0, 1, 4, 2, 3))

    cache_kt = jnp.transpose(cache_k, (0, 2, 3, 4, 1)).reshape(cache_k.shape[0], tw, page)
    cache_vh = jnp.transpose(cache_v, (0, 2, 1, 3))

    mem_k_p, mem_v_p = _mem_kv(mem_prompt, g_mem, w_mem_kv, g_mem_kn, bd)
    mem_k_s = mem_t(cache_mem_k)
    mem_v_s = mem_t(cache_mem_v)

    bias_p = _rel_bias_tiles(rel_bias, _prompt_buckets(t_attn))
    n_pages = page_table.shape[1]
    npg = next(c for c in (8, 4, 2, 1) if n_pages % c == 0)
    bias_s = jnp.swapaxes(_rel_bias_tiles(rel_bias, _sample_buckets(ss, page)), 0, 1)
    bias_s = bias_s.reshape(3, DIFF_HEADS * 2 * ss, page)
    bias_steps = jnp.stack([jnp.zeros((bias_s.shape[1], npg * page), F32),
                            jnp.pad(bias_s[1], ((0, 0), ((npg - 1) * page, 0)))])
    lam_init = 0.8 - 0.6 * math.exp(-0.3 * 1)

    def ffn(h, l, tb):
        b, s, _ = h.shape
        return _peer_ffn(h.reshape(b * s, d), g_ffn[l], w_peer_q[l], peer_subkeys[l], u_bf[l], vt_bf[l],
                         tb, PEER_EXPERT_CHUNK).reshape(b, s, d)

    n = bp * sp
    tb = _block(n, 512)
    zeros_conv = jnp.zeros((bp, 2, w_conv.shape[-1]), F32)
    h, conv_p = _layer_a(x_prompt, zeros_conv, mem_k_p[0], mem_v_p[0], g_mix[0], w_in_a[0], w_conv[0],
                         w_out_a[0], g_mem_qn[0], bd, hmask, ts=_block(sp, 512))
    h = ffn(h, 0, tb)
    k_t, v_h, kt_bf, vh_bf = _shared_kv(h, g_kv, w_kv, g_kn, bd, t_attn, True)
    h = _layer_b_prompt(h, kt_bf, vh_bf, mem_k_p[1], mem_v_p[1], g_mix[1], w_in_b[0], g_qn_b[0], lam_b[0],
                        g_subln[0], w_out_b[0], g_mem_qn[1], bd, hmask, cmask, bias_p, lam_init, t_attn)
    y_p = ffn(h, 1, tb)
    k_p = jnp.transpose(k_t.reshape(bp, DIFF_HEADS, 2, GROUP, sp), (0, 4, 1, 2, 3))
    v_p = jnp.transpose(v_h, (0, 2, 1, 3))

    n = bs * ss
    tb = _block(n, 512)
    h, conv_s = _layer_a(x_sample, state_conv[0], mem_k_s[0], mem_v_s[0], g_mix[0], w_in_a[0], w_conv[0],
                         w_out_a[0], g_mem_qn[0], bd, hmask, ts=_block(ss, 512))
    h = ffn(h, 0, tb)
    k_s, v_s = _shared_kv(h.reshape(1, n, d), g_kv, w_kv, g_kn, bd, n, False)
    k_s = k_s.reshape(bs, ss, tw)
    v_s = v_s.reshape(bs, ss, tw)
    h = _layer_b_sample(h, cache_kt, cache_vh, page_table, k_s, v_s, mem_k_s[1], mem_v_s[1], g_mix[1],
                        w_in_b[0], g_qn_b[0], lam_b[0], g_subln[0], w_out_b[0], g_mem_qn[1], bd, hmask,
                        _sample_qmask(ss), bias_steps, bias_s[2], lam_init, npg)
    y_s = ffn(h, 1, tb)

    return (y_p, y_s, conv_p[None], conv_s[None], k_p, v_p,
            k_s.reshape(bs, ss, DIFF_HEADS, 2, GROUP), v_s.reshape(bs, ss, DIFF_HEADS, 2 * GROUP),
            mem_t_inv(mem_k_p), mem_t_inv(mem_v_p))
```

```python
import functools
import math

import numpy as np
import jax
import jax.numpy as jnp
from jax import lax
from jax.experimental import pallas as pl
from jax.experimental.pallas import tpu as pltpu

F32 = jnp.float32
BF16 = jnp.bfloat16

EPS = 1e-6
NEG = -1e30
LANES = 128
GROUP = 64
MEM_HEADS = 4
DIFF_HEADS = 6
N_BUCKETS = 32
MAX_DISTANCE = 128
PEER_HEADS = 8
PEER_KEYS = 128
PEER_TOPK = 16
PEER_EXPERT_CHUNK = 2048
VMEM_LIMIT = 56 * 1024 * 1024


def _cparams(n_grid):
    return pltpu.CompilerParams(dimension_semantics=("arbitrary",) * n_grid,
                                vmem_limit_bytes=VMEM_LIMIT)


def _dot(a, b):
    return jnp.dot(a, b, preferred_element_type=F32)


def _dot_nt(a, b):
    return lax.dot_general(a, b, (((1,), (1,)), ((), ())), preferred_element_type=F32)


def _rms(x, g):
    ms = jnp.mean(x * x, axis=-1, keepdims=True)
    return x * lax.rsqrt(ms + EPS) * g


def _group_rms(x, bd, g):
    x2 = x * x
    hi = x2.astype(BF16)
    lo = (x2 - hi.astype(F32)).astype(BF16)
    parts = []
    for s in range(x.shape[-1] // 256):
        sl = slice(s * 256, (s + 1) * 256)
        parts.append(_dot(hi[:, sl], bd) + _dot(lo[:, sl], bd))
    ms = parts[0] if len(parts) == 1 else jnp.concatenate(parts, axis=-1)
    return x * lax.rsqrt(ms + EPS) * g


def _mem_attn(qm, mk_t, mv_t, bd, gq, hmask):
    q = _group_rms(qm, bd, gq) * (GROUP ** -0.5)
    out = None
    for hd in range(MEM_HEADS):
        msk = hmask[hd:hd + 1, :]
        s = _dot((q * msk).astype(BF16), mk_t)
        m = jnp.max(s, axis=-1, keepdims=True)
        p = jnp.exp(s - m)
        l = jnp.sum(p, axis=-1, keepdims=True)
        t = _dot_nt(p.astype(BF16), mv_t) * (1.0 / l) * msk
        out = t if out is None else out + t
    return out


def _block_diag_mean():
    i = np.arange(256)
    return jnp.asarray((i[:, None] // GROUP == i[None, :] // GROUP).astype(np.float32) / GROUP, BF16)


def _mem_head_mask():
    i = np.arange(MEM_HEADS * GROUP)
    return jnp.asarray((i[None, :] // GROUP == np.arange(MEM_HEADS)[:, None]).astype(np.float32))


def _sub_head_mask():
    i = np.arange(2 * GROUP)
    return jnp.asarray((i[None, :] // GROUP == np.arange(2)[:, None]).astype(np.float32))


def _bucket_np(n):
    n = np.maximum(n, 0)
    max_exact = N_BUCKETS // 2
    ratio = np.log(np.maximum(n, 1).astype(np.float32) / np.float32(max_exact)) / np.float32(
        math.log(MAX_DISTANCE / max_exact))
    large = max_exact + (ratio * np.float32(N_BUCKETS - max_exact)).astype(np.int32)
    large = np.minimum(large, N_BUCKETS - 1)
    return np.where(n < max_exact, n, large).astype(np.int32)


def _mem_kv_kernel(mem_ref, g_ref, w_ref, gk_ref, bd_ref, k_ref, v_ref):
    xn = _rms(mem_ref[0], g_ref[0]).astype(BF16)
    z = _dot(xn, w_ref[0])
    w = z.shape[-1] // 2
    k_ref[0, 0] = _group_rms(z[:, :w], bd_ref[...], gk_ref[0]).T
    v_ref[0, 0] = z[:, w:].T


def _mem_kv(mem, g_mem, w_mem_kv, g_kn, bd):
    b, nm, d = mem.shape
    depth = g_mem.shape[0]
    w = w_mem_kv.shape[-1] // 2
    out = jax.ShapeDtypeStruct((depth, b, w, nm), F32)
    return pl.pallas_call(
        _mem_kv_kernel,
        grid=(depth, b),
        in_specs=[
            pl.BlockSpec((1, nm, d), lambda l, i: (i, 0, 0)),
            pl.BlockSpec((1, 1, d), lambda l, i: (l, 0, 0)),
            pl.BlockSpec((1, d, 2 * w), lambda l, i: (l, 0, 0)),
            pl.BlockSpec((1, 1, w), lambda l, i: (l, 0, 0)),
            pl.BlockSpec((256, 256), lambda l, i: (0, 0)),
        ],
        out_specs=[pl.BlockSpec((1, 1, w, nm), lambda l, i: (l, i, 0, 0))] * 2,
        out_shape=[out, out],
        compiler_params=_cparams(2),
        name="mem_kv",
    )(mem, g_mem.reshape(depth, 1, d), w_mem_kv.astype(BF16),
      jnp.tile(g_kn, (1, MEM_HEADS)).reshape(depth, 1, w), bd)


def _layer_a_kernel(h_ref, prev_ref, mk_ref, mv_ref, gmix_ref, win_ref, wconv_ref, wout_ref, gq_ref,
                    bd_ref, hmask_ref, out_ref, conv_ref, ubuf):
    j = pl.program_id(1)
    ts = h_ref.shape[1]
    tw = wconv_ref.shape[1]
    h = h_ref[0]
    xn = _rms(h, gmix_ref[...]).astype(BF16)
    z = _dot(xn, win_ref[...])
    u = z[:, tw:2 * tw] * z[:, 2 * tw:3 * tw]

    @pl.when(j == 0)
    def _():
        ubuf[6:8, :] = prev_ref[0]

    @pl.when(j > 0)
    def _():
        ubuf[6:8, :] = ubuf[ts + 6:ts + 8, :]

    ubuf[8:8 + ts, :] = u
    conv = (wconv_ref[0:1, :] * ubuf[6:6 + ts, :] + wconv_ref[1:2, :] * ubuf[7:7 + ts, :]
            + wconv_ref[2:3, :] * u)
    conv_ref[0] = ubuf[ts + 6:ts + 8, :]
    tok = z[:, :tw] * conv
    mo = _mem_attn(z[:, 3 * tw:], mk_ref[0].astype(BF16), mv_ref[0].astype(BF16), bd_ref[...],
                   gq_ref[...], hmask_ref[...])
    out_ref[0] = (h + _dot(tok.astype(BF16), wout_ref[0:tw, :])
                  + _dot(mo.astype(BF16), wout_ref[tw:, :]))


def _layer_a(h, conv_prev, mem_k, mem_v, g_mix, w_in, w_conv, w_out, g_qn, bd, hmask, ts):
    b, s, d = h.shape
    tw = w_conv.shape[1]
    mw = mem_k.shape[1]
    nm = mem_k.shape[2]
    const = lambda i, j: (0, 0)
    return pl.pallas_call(
        _layer_a_kernel,
        grid=(b, s // ts),
        in_specs=[
            pl.BlockSpec((1, ts, d), lambda i, j: (i, j, 0)),
            pl.BlockSpec((1, 2, tw), lambda i, j: (i, 0, 0)),
            pl.BlockSpec((1, mw, nm), lambda i, j: (i, 0, 0)),
            pl.BlockSpec((1, mw, nm), lambda i, j: (i, 0, 0)),
            pl.BlockSpec((1, d), const),
            pl.BlockSpec(w_in.shape, const),
            pl.BlockSpec(w_conv.shape, const),
            pl.BlockSpec(w_out.shape, const),
            pl.BlockSpec((1, mw), const),
            pl.BlockSpec((256, 256), const),
            pl.BlockSpec(hmask.shape, const),
        ],
        out_specs=[pl.BlockSpec((1, ts, d), lambda i, j: (i, j, 0)),
                   pl.BlockSpec((1, 2, tw), lambda i, j: (i, 0, 0))],
        out_shape=[jax.ShapeDtypeStruct((b, s, d), F32), jax.ShapeDtypeStruct((b, 2, tw), F32)],
        scratch_shapes=[pltpu.VMEM((ts + 8, tw), F32)],
        compiler_params=_cparams(2),
        name="layer_a",
    )(h, conv_prev, mem_k, mem_v, g_mix.reshape(1, d), w_in.astype(BF16), w_conv, w_out.astype(BF16),
      jnp.tile(g_qn, MEM_HEADS).reshape(1, mw), bd, hmask)


def _top16_rows(v, with_levels=False):
    rows = []
    levels = jnp.zeros(v.shape, F32) if with_levels else None
    for r in range(PEER_TOPK):
        m = jnp.max(v, axis=0, keepdims=True)
        rows.append(m)
        if with_levels or r + 1 < PEER_TOPK:
            hit = v == m
            if with_levels:
                levels = jnp.where(hit, float(PEER_TOPK - r), levels)
            if r + 1 < PEER_TOPK:
                v = jnp.where(hit, -jnp.inf, v)
    rows = jnp.concatenate(rows, axis=0)
    return (rows, levels) if with_levels else rows


def _peer_pre_kernel(h_ref, g_ref, wq_ref, sk_ref, hnT_ref, k1_ref, e1_ref, kap_ref, e0_ref, sc_scr):
    tb = h_ref.shape[0]
    ncol = tb // LANES
    hn = _rms(h_ref[...], g_ref[...])
    hnT_ref[...] = hn.T.astype(BF16)
    q = _dot(hn.astype(BF16), wq_ref[...]).astype(BF16)
    for hc in range(2 * PEER_HEADS):
        sc_t = _dot_nt(sk_ref[hc % 2], q[:, hc * LANES:(hc + 1) * LANES])
        for col in range(ncol):
            sc_scr[hc, col] = sc_t[:, col * LANES:(col + 1) * LANES]

    def body(idx, carry):
        hd = idx // ncol
        col = idx % ncol
        s0 = sc_scr[2 * hd, col]
        s1 = sc_scr[2 * hd + 1, col]
        a = _top16_rows(s0)
        b, k1 = _top16_rows(s1, with_levels=True)
        cand = jnp.concatenate(
            [a[0:1] + b] + [a[r:r + 1] + b[0:8] for r in range(1, 8)] + [a[8:16] + b[0:1]], axis=0)
        v = cand
        for _ in range(PEER_TOPK - 1):
            m = jnp.max(v, axis=0, keepdims=True)
            v = jnp.where(v == m, -jnp.inf, v)
        t = jnp.max(v, axis=0, keepdims=True)
        top = a[0:1] + b[0:1]
        zsum = jnp.sum(jnp.where(cand >= t, jnp.exp(cand - top), 0.0), axis=0, keepdims=True)
        nq = jnp.zeros(s0.shape, F32)
        for c in range(PEER_TOPK):
            nq = nq + jnp.where(s0 + b[c:c + 1] >= t, 1.0, 0.0)
        kap = jnp.where(s0 >= a[PEER_TOPK - 1:PEER_TOPK], (PEER_TOPK + 1.0) - nq, PEER_TOPK + 1.0)
        k1_ref[hd, col] = pltpu.bitcast(k1.astype(BF16), jnp.uint32)
        e1_ref[hd, col] = pltpu.bitcast((jnp.exp(s1 - b[0:1]) * (1.0 / zsum)).astype(BF16), jnp.uint32)
        kap_ref[hd, col] = kap
        e0_ref[hd, col] = jnp.exp(s0 - a[0:1])
        return carry

    lax.fori_loop(0, PEER_HEADS * ncol, body, 0, unroll=2)


def _packed_rows():
    return PEER_KEYS * jnp.dtype(BF16).itemsize // 4


def _peer_pre(h2d, g_ffn, w_q, sub_keys, tb):
    n, d = h2d.shape
    ncols = n // LANES
    ncol = tb // LANES
    const2 = lambda i: (0, 0)
    prow = _packed_rows()
    tiles = ([jax.ShapeDtypeStruct((PEER_HEADS, ncols, prow, LANES), jnp.uint32)] * 2
             + [jax.ShapeDtypeStruct((PEER_HEADS, ncols, PEER_KEYS, LANES), F32)] * 2)
    packed_spec = pl.BlockSpec((PEER_HEADS, ncol, prow, LANES), lambda i: (0, i, 0, 0))
    tile_spec = pl.BlockSpec((PEER_HEADS, ncol, PEER_KEYS, LANES), lambda i: (0, i, 0, 0))
    return pl.pallas_call(
        _peer_pre_kernel,
        grid=(n // tb,),
        in_specs=[
            pl.BlockSpec((tb, d), lambda i: (i, 0)),
            pl.BlockSpec((1, d), const2),
            pl.BlockSpec(w_q.shape, const2),
            pl.BlockSpec(sub_keys.shape, lambda i: (0, 0, 0)),
        ],
        out_specs=[pl.BlockSpec((d, tb), lambda i: (0, i))] + [packed_spec] * 2 + [tile_spec] * 2,
        out_shape=[jax.ShapeDtypeStruct((d, n), BF16)] + tiles,
        scratch_shapes=[pltpu.VMEM((2 * PEER_HEADS, ncol, PEER_KEYS, LANES), F32)],
        compiler_params=_cparams(1),
        name="peer_pre",
    )(h2d, g_ffn.reshape(1, d), w_q.astype(BF16), sub_keys.astype(BF16))


GATE_ROW_SPLIT = 2


def _row_tile_bf16(ref, hd, col, i, nrow):
    row = jnp.broadcast_to(ref[hd, col, pl.ds(i, 1), :], (16, LANES)).astype(BF16)
    return jnp.concatenate([row] * (nrow // 16), axis=0)


def _gate_slab(a_ref, p_ref, sub, i, k1_ref, e1_ref, kap_ref, e0_ref):
    sqrt_half = np.float32(np.sqrt(0.5))
    nrow = PEER_KEYS // GATE_ROW_SPLIT
    wrow = nrow * _packed_rows() // PEER_KEYS
    for col in range(a_ref.shape[2] // LANES):
        cs = slice(col * LANES, (col + 1) * LANES)
        for part in range(GATE_ROW_SPLIT):
            w = None
            for hd in range(PEER_HEADS):
                kap = _row_tile_bf16(kap_ref, hd, col, i, nrow)
                e0 = _row_tile_bf16(e0_ref, hd, col, i, nrow)
                k1 = pltpu.bitcast(k1_ref[hd, col, part * wrow:(part + 1) * wrow, :], BF16)
                e1 = pltpu.bitcast(e1_ref[hd, col, part * wrow:(part + 1) * wrow, :], BF16)
                t = jnp.where(k1 >= kap, e1, jnp.zeros((), BF16)) * e0
                w = t if w is None else w + t
            a = a_ref[sub, part * nrow:(part + 1) * nrow, cs]
            act = 0.5 * a * (1.0 + lax.erf(a * sqrt_half))
            p_ref[sub, part * nrow:(part + 1) * nrow, cs] = w * act.astype(BF16)


def _peer_dense_kernel(hnT_ref, k1_ref, e1_ref, kap_ref, e0_ref, u_ref, vt_ref, h_ref, out_ref, acc, a_scr, p_scr):
    k = pl.program_id(1)
    nsub = a_scr.shape[0]

    @pl.when(k == 0)
    def _():
        acc[...] = jnp.zeros_like(acc)

    a_scr[...] = _dot(u_ref[...], hnT_ref[...]).reshape(a_scr.shape)

    def body(sub, carry):
        _gate_slab(a_scr, p_scr, sub, k * nsub + sub, k1_ref, e1_ref, kap_ref, e0_ref)
        return carry

    lax.fori_loop(0, nsub, body, 0)
    acc[...] += _dot(vt_ref[...], p_scr[...].reshape(nsub * LANES, p_scr.shape[2]))

    @pl.when(k == pl.num_programs(1) - 1)
    def _():
        out_ref[...] = h_ref[...] + acc[...].T


def _peer_dense(h2d, hnT, k1, e1, kap, e0, u_bf, vt_bf, layer, tb, ec):
    n, d = h2d.shape
    ne = u_bf.shape[1]
    ncol = tb // LANES
    packed_spec = pl.BlockSpec((PEER_HEADS, ncol, _packed_rows(), LANES), lambda t, k: (0, t, 0, 0))
    tile_spec = pl.BlockSpec((PEER_HEADS, ncol, PEER_KEYS, LANES), lambda t, k: (0, t, 0, 0))
    return pl.pallas_call(
        _peer_dense_kernel,
        grid=(n // tb, ne // ec),
        in_specs=[pl.BlockSpec((d, tb), lambda t, k: (0, t))] + [packed_spec] * 2 + [tile_spec] * 2 + [
            pl.BlockSpec((None, ec, d), lambda t, k: (layer, k, 0)),
            pl.BlockSpec((None, d, ec), lambda t, k: (layer, 0, k)),
            pl.BlockSpec((tb, d), lambda t, k: (t, 0)),
        ],
        out_specs=pl.BlockSpec((tb, d), lambda t, k: (t, 0)),
        out_shape=jax.ShapeDtypeStruct((n, d), F32),
        scratch_shapes=[pltpu.VMEM((d, tb), F32),
                        pltpu.VMEM((ec // LANES, LANES, tb), F32),
                        pltpu.VMEM((ec // LANES, LANES, tb), BF16)],
        compiler_params=_cparams(2),
        name="peer_dense",
    )(hnT, k1, e1, kap, e0, u_bf, vt_bf, h2d)


def _peer_ffn(h2d, g_ffn, w_q, sub_keys, u_bf, vt_bf, layer, tb, ec):
    hnT, k1, e1, kap, e0 = _peer_pre(h2d, g_ffn, w_q, sub_keys, tb)
    return _peer_dense(h2d, hnT, k1, e1, kap, e0, u_bf, vt_bf, layer, tb, ec)


def _shared_kv_kernel(head_major, h_ref, g_ref, w_ref, gk_ref, bd_ref, k_ref, v_ref, *bf_refs):
    tw = gk_ref.shape[-1]
    xn = _rms(h_ref[0], g_ref[...]).astype(BF16)
    z = _dot(xn, w_ref[...])
    k = _group_rms(z[:, :tw], bd_ref[...], gk_ref[...])
    v = z[:, tw:]
    if not head_major:
        k_ref[0] = k
        v_ref[0] = v
        return
    kt_ref, vh_ref = bf_refs
    k_t = k.T
    k_ref[0] = k_t
    for hd in range(DIFF_HEADS):
        sl = slice(hd * LANES, (hd + 1) * LANES)
        kt_ref[0, hd] = k_t[sl, :].astype(BF16)
        v_ref[0, hd] = v[:, sl]
        vh_ref[0, hd] = v[:, sl].astype(BF16)


def _shared_kv(h, g_kv, w_kv, g_kn, bd, ts, head_major):
    b, s, d = h.shape
    tw = DIFF_HEADS * 2 * GROUP
    const = lambda i, j: (0, 0)
    if head_major:
        out_specs = [pl.BlockSpec((1, tw, ts), lambda i, j: (i, 0, j)),
                     pl.BlockSpec((1, DIFF_HEADS, ts, LANES), lambda i, j: (i, 0, j, 0)),
                     pl.BlockSpec((1, DIFF_HEADS, LANES, ts), lambda i, j: (i, 0, 0, j)),
                     pl.BlockSpec((1, DIFF_HEADS, ts, LANES), lambda i, j: (i, 0, j, 0))]
        out_shape = [jax.ShapeDtypeStruct((b, tw, s), F32),
                     jax.ShapeDtypeStruct((b, DIFF_HEADS, s, LANES), F32),
                     jax.ShapeDtypeStruct((b, DIFF_HEADS, LANES, s), BF16),
                     jax.ShapeDtypeStruct((b, DIFF_HEADS, s, LANES), BF16)]
    else:
        out_specs = [pl.BlockSpec((1, ts, tw), lambda i, j: (i, j, 0))] * 2
        out_shape = [jax.ShapeDtypeStruct((b, s, tw), F32)] * 2
    return pl.pallas_call(
        functools.partial(_shared_kv_kernel, head_major),
        grid=(b, s // ts),
        in_specs=[
            pl.BlockSpec((1, ts, d), lambda i, j: (i, j, 0)),
            pl.BlockSpec((1, d), const),
            pl.BlockSpec(w_kv.shape, const),
            pl.BlockSpec((1, tw), const),
            pl.BlockSpec((256, 256), const),
        ],
        out_specs=out_specs,
        out_shape=out_shape,
        compiler_params=_cparams(2),
        name="shared_kv",
    )(h, g_kv.reshape(1, d), w_kv.astype(BF16), jnp.tile(g_kn, 2 * DIFF_HEADS).reshape(1, tw), bd)


def _rel_bias_kernel(rb_ref, bucket_ref, out_ref):
    hd = pl.program_id(0)
    bk = bucket_ref[0]
    far = rb_ref[N_BUCKETS - 1, hd]
    acc = jnp.zeros(bk.shape, F32)
    for n in range(N_BUCKETS - 1):
        acc = jnp.where(bk == n, rb_ref[n, hd] - far, acc)
    out_ref[0, 0] = jnp.where(bk < 0, NEG, acc)


def _rel_bias_tiles(rel_bias, buckets):
    kinds, r, c = buckets.shape
    return pl.pallas_call(
        _rel_bias_kernel,
        grid=(DIFF_HEADS, kinds),
        in_specs=[pl.BlockSpec(memory_space=pltpu.SMEM),
                  pl.BlockSpec((1, r, c), lambda h, k: (k, 0, 0))],
        out_specs=pl.BlockSpec((1, 1, r, c), lambda h, k: (h, k, 0, 0)),
        out_shape=jax.ShapeDtypeStruct((DIFF_HEADS, kinds, r, c), F32),
        compiler_params=_cparams(2),
        name="rel_bias_tiles",
    )(rel_bias, jnp.asarray(buckets))


def _lambda_value(lam_ref, lam_init):
    lp = lam_ref[...]
    s01 = jnp.sum(lp[0:1] * lp[1:2], axis=-1, keepdims=True)
    s23 = jnp.sum(lp[2:3] * lp[3:4], axis=-1, keepdims=True)
    return jnp.exp(s01) - jnp.exp(s23) + lam_init


def _layer_b_prompt_kernel(lam_init, h_ref, kt_ref, vh_ref, mk_ref, mv_ref, gmix_ref, win_ref, gq_ref,
                           lam_ref, gsub_ref, wout_ref, gmq_ref, bd_ref, hmask_ref, cmask_ref, bias_ref,
                           out_ref, q_scr, o_scr):
    qb = pl.program_id(1)
    t = h_ref.shape[1]
    tw = DIFF_HEADS * LANES
    h = h_ref[0]
    xn = _rms(h, gmix_ref[...]).astype(BF16)
    z = _dot(xn, win_ref[...])
    q = _group_rms(z[:, :tw], bd_ref[...], gq_ref[...]) * (GROUP ** -0.5)
    for hd in range(DIFF_HEADS):
        qh = q[:, hd * LANES:(hd + 1) * LANES]
        for c in range(2):
            q_scr[2 * hd + c] = (qh * cmask_ref[c:c + 1, :]).astype(BF16)
    lam = _lambda_value(lam_ref, lam_init)

    def attend(nk):
        w = nk * t

        def head_body(hd, carry):
            outs = []
            for c in range(2):
                s = _dot(q_scr[2 * hd + c], kt_ref[0, hd, :, 0:w])
                near = [s[:, w - t:] + bias_ref[hd, 0]]
                if nk > 1:
                    near = [s[:, w - 2 * t:w - t] + bias_ref[hd, 1]] + near
                if nk > 2:
                    near = [s[:, :w - 2 * t]] + near
                s = near[0] if len(near) == 1 else jnp.concatenate(near, axis=-1)
                p = jnp.exp(s - jnp.max(s, axis=-1, keepdims=True))
                l = jnp.sum(p, axis=-1, keepdims=True)
                outs.append(_dot(p.astype(BF16), vh_ref[0, hd, 0:w, :]) * (1.0 / l))
            o = outs[0] - lam * outs[1]
            o_scr[hd] = _rms(o, gsub_ref[...]) * (1.0 - lam_init)
            return carry

        lax.fori_loop(0, DIFF_HEADS, head_body, 0)

    for nk in range(1, kt_ref.shape[3] // t + 1):
        pl.when(qb == nk - 1)(functools.partial(attend, nk))

    tok = jnp.concatenate([o_scr[hd] for hd in range(DIFF_HEADS)], axis=-1)
    mo = _mem_attn(z[:, tw:], mk_ref[0].astype(BF16), mv_ref[0].astype(BF16), bd_ref[...],
                   gmq_ref[...], hmask_ref[...])
    out_ref[0] = (h + _dot(tok.astype(BF16), wout_ref[0:tw, :])
                  + _dot(mo.astype(BF16), wout_ref[tw:, :]))


def _layer_b_prompt(h, kt, vh, mem_k, mem_v, g_mix, w_in, g_qn, lam_b, g_subln, w_out, g_mqn, bd, hmask,
                    cmask, bias, lam_init, t):
    b, s, d = h.shape
    tw = DIFF_HEADS * LANES
    mw = mem_k.shape[1]
    nm = mem_k.shape[2]
    const = lambda i, j: (0, 0)
    return pl.pallas_call(
        functools.partial(_layer_b_prompt_kernel, lam_init),
        grid=(b, s // t),
        in_specs=[
            pl.BlockSpec((1, t, d), lambda i, j: (i, j, 0)),
            pl.BlockSpec((1,) + kt.shape[1:], lambda i, j: (i, 0, 0, 0)),
            pl.BlockSpec((1,) + vh.shape[1:], lambda i, j: (i, 0, 0, 0)),
            pl.BlockSpec((1, mw, nm), lambda i, j: (i, 0, 0)),
            pl.BlockSpec((1, mw, nm), lambda i, j: (i, 0, 0)),
            pl.BlockSpec((1, d), const),
            pl.BlockSpec(w_in.shape, const),
            pl.BlockSpec((1, tw), const),
            pl.BlockSpec(lam_b.shape, const),
            pl.BlockSpec((1, LANES), const),
            pl.BlockSpec(w_out.shape, const),
            pl.BlockSpec((1, mw), const),
            pl.BlockSpec((256, 256), const),
            pl.BlockSpec(hmask.shape, const),
            pl.BlockSpec(cmask.shape, const),
            pl.BlockSpec(bias.shape, lambda i, j: (0, 0, 0, 0)),
        ],
        out_specs=pl.BlockSpec((1, t, d), lambda i, j: (i, j, 0)),
        out_shape=jax.ShapeDtypeStruct((b, s, d), F32),
        scratch_shapes=[pltpu.VMEM((2 * DIFF_HEADS, t, LANES), BF16),
                        pltpu.VMEM((DIFF_HEADS, t, LANES), F32)],
        compiler_params=_cparams(2),
        name="layer_b_prompt",
    )(h, kt, vh, mem_k, mem_v, g_mix.reshape(1, d), w_in.astype(BF16),
      jnp.tile(g_qn, 2 * DIFF_HEADS).reshape(1, tw), lam_b, g_subln.reshape(1, LANES),
      w_out.astype(BF16), jnp.tile(g_mqn, MEM_HEADS).reshape(1, mw), bd, hmask, cmask, bias)


def _layer_b_sample_kernel(lam_init, npg, pt_ref, h_ref, *refs):
    del pt_ref
    ck_refs, cv_refs = refs[:npg], refs[npg:2 * npg]
    (kn_ref, vn_ref, mk_ref, mv_ref, gmix_ref, win_ref, gq_ref, lam_ref, gsub_ref, wout_ref, gmq_ref, bd_ref,
     hmask_ref, qmask_ref, bias_ref, bias_new_ref, out_ref, qbd, qm_scr, m_scr, l_scr, acc, kv_scr) = refs[2 * npg:]
    p = pl.program_id(1)
    last = pl.num_programs(1) - 1
    sq = h_ref.shape[1]
    tw = DIFF_HEADS * LANES
    rows = qbd.shape[0]

    @pl.when(p == 0)
    def _():
        xn = _rms(h_ref[0], gmix_ref[...]).astype(BF16)
        z = _dot(xn, win_ref[...])
        q = _group_rms(z[:, :tw], bd_ref[...], gq_ref[...]) * (GROUP ** -0.5)
        qbd[...] = (jnp.concatenate([q] * (rows // sq), axis=0) * qmask_ref[...]).astype(BF16)
        qm_scr[...] = z[:, tw:]
        m_scr[...] = jnp.full(m_scr.shape, NEG, F32)
        l_scr[...] = jnp.zeros_like(l_scr)
        acc[...] = jnp.zeros_like(acc)

    def update(s, v_heads):
        m = m_scr[...]
        m_new = jnp.maximum(m, jnp.max(s, axis=-1, keepdims=True))
        alpha = jnp.exp(m - m_new)
        pr = jnp.exp(s - m_new)
        l_scr[...] = alpha * l_scr[...] + jnp.sum(pr, axis=-1, keepdims=True)
        pr = pr.astype(BF16)
        pv = [_dot(pr[hd * 2 * sq:(hd + 1) * 2 * sq], v_heads[hd]) for hd in range(DIFF_HEADS)]
        acc[...] = alpha * acc[...] + jnp.concatenate(pv, axis=0)
        m_scr[...] = m_new

    kind = jnp.where(p == last, 1, 0)
    kt = jnp.concatenate([ck_refs[t][0].astype(BF16) for t in range(npg)], axis=1)
    update(_dot(qbd[...], kt) + bias_ref[kind],
           [jnp.concatenate([cv_refs[t][0, hd].astype(BF16) for t in range(npg)], axis=0)
            for hd in range(DIFF_HEADS)])

    @pl.when(p == last)
    def _():
        kv_scr[...] = jnp.zeros_like(kv_scr)
        kv_scr[0, 0:sq, :] = kn_ref[0]
        kv_scr[1, 0:sq, :] = vn_ref[0]
        v_new = kv_scr[1].astype(BF16)
        update(_dot_nt(qbd[...], kv_scr[0].astype(BF16)) + bias_new_ref[...],
               [v_new[:, hd * LANES:(hd + 1) * LANES] for hd in range(DIFF_HEADS)])
        lam = _lambda_value(lam_ref, lam_init)
        o_all = acc[...] * (1.0 / l_scr[...])
        toks = []
        for hd in range(DIFF_HEADS):
            r0 = hd * 2 * sq
            o = o_all[r0:r0 + sq] - lam * o_all[r0 + sq:r0 + 2 * sq]
            toks.append(_rms(o, gsub_ref[...]) * (1.0 - lam_init))
        tok = jnp.concatenate(toks, axis=-1)
        mo = _mem_attn(qm_scr[...], mk_ref[0].astype(BF16), mv_ref[0].astype(BF16), bd_ref[...],
                       gmq_ref[...], hmask_ref[...])
        out_ref[0] = (h_ref[0] + _dot(tok.astype(BF16), wout_ref[0:tw, :])
                      + _dot(mo.astype(BF16), wout_ref[tw:, :]))


def _layer_b_sample(h, cache_kt, cache_vh, page_table, k_new, v_new, mem_k, mem_v, g_mix, w_in, g_qn, lam_b,
                    g_subln, w_out, g_mqn, bd, hmask, qmask, bias, bias_new, lam_init, npg):
    b, sq, d = h.shape
    n_pages = page_table.shape[1]
    page = cache_kt.shape[2]
    tw = DIFF_HEADS * LANES
    mw = mem_k.shape[1]
    nm = mem_k.shape[2]
    rows = 2 * DIFF_HEADS * sq
    const = lambda i, j, pt: (0, 0)
    per_b = lambda i, j, pt: (i, 0, 0)

    def page_map(t, nd):
        return lambda i, j, pt: (pt[i, j * npg + t],) + (0,) * nd

    grid_spec = pltpu.PrefetchScalarGridSpec(
        num_scalar_prefetch=1,
        grid=(b, n_pages // npg),
        in_specs=[pl.BlockSpec((1, sq, d), per_b)]
        + [pl.BlockSpec((1, tw, page), page_map(t, 2)) for t in range(npg)]
        + [pl.BlockSpec((1, DIFF_HEADS, page, LANES), page_map(t, 3)) for t in range(npg)]
        + [
            pl.BlockSpec((1, sq, tw), per_b),
            pl.BlockSpec((1, sq, tw), per_b),
            pl.BlockSpec((1, mw, nm), per_b),
            pl.BlockSpec((1, mw, nm), per_b),
            pl.BlockSpec((1, d), const),
            pl.BlockSpec(w_in.shape, const),
            pl.BlockSpec((1, tw), const),
            pl.BlockSpec(lam_b.shape, const),
            pl.BlockSpec((1, LANES), const),
            pl.BlockSpec(w_out.shape, const),
            pl.BlockSpec((1, mw), const),
            pl.BlockSpec((256, 256), const),
            pl.BlockSpec(hmask.shape, const),
            pl.BlockSpec(qmask.shape, const),
            pl.BlockSpec(bias.shape, lambda i, j, pt: (0, 0, 0)),
            pl.BlockSpec(bias_new.shape, const),
        ],
        out_specs=pl.BlockSpec((1, sq, d), per_b),
        scratch_shapes=[pltpu.VMEM((rows, tw), BF16),
                        pltpu.VMEM((sq, mw), F32),
                        pltpu.VMEM((rows, 1), F32),
                        pltpu.VMEM((rows, 1), F32),
                        pltpu.VMEM((rows, LANES), F32),
                        pltpu.VMEM((2, page, tw), F32)],
    )
    return pl.pallas_call(
        functools.partial(_layer_b_sample_kernel, lam_init, npg),
        grid_spec=grid_spec,
        out_shape=jax.ShapeDtypeStruct((b, sq, d), F32),
        compiler_params=_cparams(2),
        name="layer_b_sample",
    )(page_table, h, *([cache_kt] * npg), *([cache_vh] * npg), k_new, v_new, mem_k, mem_v,
      g_mix.reshape(1, d), w_in.astype(BF16), jnp.tile(g_qn, 2 * DIFF_HEADS).reshape(1, tw), lam_b,
      g_subln.reshape(1, LANES), w_out.astype(BF16), jnp.tile(g_mqn, MEM_HEADS).reshape(1, mw), bd, hmask,
      qmask, bias, bias_new)


def _prompt_buckets(t):
    r = np.arange(t)[:, None]
    c = np.arange(t)[None, :]
    diag = np.where(c <= r, _bucket_np(r - c), -1)
    sub = _bucket_np(t + r - c)
    return np.stack([diag, sub]).astype(np.int32)


def _sample_buckets(sq, page):
    qi = np.tile(np.arange(sq), 2)[:, None]
    tok = np.arange(page)[None, :]
    far = np.full((2 * sq, page), N_BUCKETS - 1, np.int32)
    last = _bucket_np(page + qi - tok)
    new = np.where(tok <= qi, _bucket_np(qi - tok), -1)
    return np.stack([far, last, new]).astype(np.int32)


def _sample_qmask(sq):
    rows = np.arange(2 * DIFF_HEADS * sq)[:, None] // sq
    lanes = np.arange(DIFF_HEADS * LANES)[None, :] // GROUP
    return jnp.asarray((rows == lanes).astype(np.float32))


def _block(n, pref):
    return pref if n % pref == 0 else n


def kernel(x_prompt, x_sample, state_conv, cache_k, cache_v, cache_mem_k, cache_mem_v, page_table, mem_prompt, g_mix, g_ffn, w_in_a, w_conv, w_out_a, w_in_b, g_qn_b, lam_b, g_subln, w_out_b, g_kv, w_kv, g_kn, rel_bias, g_mem, w_mem_kv, g_mem_qn, g_mem_kn, w_peer_q, peer_subkeys, peer_u, peer_v):
    depth = g_mix.shape[0]
    n_a = w_in_a.shape[0]
    assert depth == 2 and n_a == 1, "one short-conv layer followed by one differential-attention layer"
    bp, sp, d = x_prompt.shape
    bs, ss, _ = x_sample.shape
    nm = mem_prompt.shape[1]
    mw = MEM_HEADS * GROUP
    tw = DIFF_HEADS * 2 * GROUP
    page = cache_k.shape[1]
    t_attn = _block(sp, 256)

    bd = _block_diag_mean()
    hmask = _mem_head_mask()
    cmask = _sub_head_mask()
    u_bf = peer_u.astype(BF16)
    vt_bf = jnp.swapaxes(peer_v.astype(BF16), 1, 2)

    def mem_t(x):
        return jnp.transpose(x, (0, 1, 3, 4, 2)).reshape(depth, x.shape[1], mw, nm)

    def mem_t_inv(x):
        return jnp.transpose(x.reshape(depth, x.shape[1], MEM_HEADS, GROUP, nm), (0, 1, 4, 2, 3))

    cache_kt = jnp.transpose(cache_k, (0, 2, 3, 4, 1)).reshape(cache_k.shape[0], tw, page)
    cache_vh = jnp.transpose(cache_v, (0, 2, 1, 3))

    mem_k_p, mem_v_p = _mem_kv(mem_prompt, g_mem, w_mem_kv, g_mem_kn, bd)
    mem_k_s = mem_t(cache_mem_k)
    mem_v_s = mem_t(cache_mem_v)

    bias_p = _rel_bias_tiles(rel_bias, _prompt_buckets(t_attn))
    n_pages = page_table.shape[1]
    npg = next(c for c in (8, 4, 2, 1) if n_pages % c == 0)
    bias_s = jnp.swapaxes(_rel_bias_tiles(rel_bias, _sample_buckets(ss, page)), 0, 1)
    bias_s = bias_s.reshape(3, DIFF_HEADS * 2 * ss, page)
    bias_steps = jnp.stack([jnp.zeros((bias_s.shape[1], npg * page), F32),
                            jnp.pad(bias_s[1], ((0, 0), ((npg - 1) * page, 0)))])
    lam_init = 0.8 - 0.6 * math.exp(-0.3 * 1)

    def ffn(h, l, tb):
        b, s, _ = h.shape
        return _peer_ffn(h.reshape(b * s, d), g_ffn[l], w_peer_q[l], peer_subkeys[l], u_bf, vt_bf, l,
                         tb, PEER_EXPERT_CHUNK).reshape(b, s, d)

    n = bp * sp
    tb = _block(n, 512)
    zeros_conv = jnp.zeros((bp, 2, w_conv.shape[-1]), F32)
    h, conv_p = _layer_a(x_prompt, zeros_conv, mem_k_p[0], mem_v_p[0], g_mix[0], w_in_a[0], w_conv[0],
                         w_out_a[0], g_mem_qn[0], bd, hmask, ts=_block(sp, 512))
    h = ffn(h, 0, tb)
    k_t, v_h, kt_bf, vh_bf = _shared_kv(h, g_kv, w_kv, g_kn, bd, t_attn, True)
    h = _layer_b_prompt(h, kt_bf, vh_bf, mem_k_p[1], mem_v_p[1], g_mix[1], w_in_b[0], g_qn_b[0], lam_b[0],
                        g_subln[0], w_out_b[0], g_mem_qn[1], bd, hmask, cmask, bias_p, lam_init, t_attn)
    y_p = ffn(h, 1, tb)
    k_p = jnp.transpose(k_t.reshape(bp, DIFF_HEADS, 2, GROUP, sp), (0, 4, 1, 2, 3))
    v_p = jnp.transpose(v_h, (0, 2, 1, 3))

    n = bs * ss
    tb = _block(n, 512)
    h, conv_s = _layer_a(x_sample, state_conv[0], mem_k_s[0], mem_v_s[0], g_mix[0], w_in_a[0], w_conv[0],
                         w_out_a[0], g_mem_qn[0], bd, hmask, ts=_block(ss, 512))
    h = ffn(h, 0, tb)
    k_s, v_s = _shared_kv(h.reshape(1, n, d), g_kv, w_kv, g_kn, bd, n, False)
    k_s = k_s.reshape(bs, ss, tw)
    v_s = v_s.reshape(bs, ss, tw)
    h = _layer_b_sample(h, cache_kt, cache_vh, page_table, k_s, v_s, mem_k_s[1], mem_v_s[1], g_mix[1],
                        w_in_b[0], g_qn_b[0], lam_b[0], g_subln[0], w_out_b[0], g_mem_qn[1], bd, hmask,
                        _sample_qmask(ss), bias_steps, bias_s[2], lam_init, npg)
    y_s = ffn(h, 1, tb)

    return (y_p, y_s, conv_p[None], conv_s[None], k_p, v_p,
            k_s.reshape(bs, ss, DIFF_HEADS, 2, GROUP), v_s.reshape(bs, ss, DIFF_HEADS, 2 * GROUP),
            mem_t_inv(mem_k_p), mem_t_inv(mem_v_p))
```

```python
import functools
import math

import numpy as np
import jax
import jax.numpy as jnp
from jax import lax
from jax.experimental import pallas as pl
from jax.experimental.pallas import tpu as pltpu

F32 = jnp.float32
BF16 = jnp.bfloat16

EPS = 1e-6
NEG = -1e30
LANES = 128
GROUP = 64
MEM_HEADS = 4
DIFF_HEADS = 6
N_BUCKETS = 32
MAX_DISTANCE = 128
PEER_HEADS = 8
PEER_KEYS = 128
PEER_TOPK = 16
PEER_EXPERT_CHUNK = 2048
VMEM_LIMIT = 56 * 1024 * 1024


def _cparams(n_grid):
    return pltpu.CompilerParams(dimension_semantics=("arbitrary",) * n_grid,
                                vmem_limit_bytes=VMEM_LIMIT)


def _dot(a, b):
    return jnp.dot(a, b, preferred_element_type=F32)


def _dot_nt(a, b):
    return lax.dot_general(a, b, (((1,), (1,)), ((), ())), preferred_element_type=F32)


def _rms(x, g):
    ms = jnp.mean(x * x, axis=-1, keepdims=True)
    return x * lax.rsqrt(ms + EPS) * g


def _group_rms(x, bd, g):
    x2 = x * x
    hi = x2.astype(BF16)
    lo = (x2 - hi.astype(F32)).astype(BF16)
    parts = []
    for s in range(x.shape[-1] // 256):
        sl = slice(s * 256, (s + 1) * 256)
        parts.append(_dot(hi[:, sl], bd) + _dot(lo[:, sl], bd))
    ms = parts[0] if len(parts) == 1 else jnp.concatenate(parts, axis=-1)
    return x * lax.rsqrt(ms + EPS) * g


def _mem_attn(qm, mk_t, mv_t, bd, gq, hmask):
    q = _group_rms(qm, bd, gq) * (GROUP ** -0.5)
    out = None
    for hd in range(MEM_HEADS):
        msk = hmask[hd:hd + 1, :]
        s = _dot((q * msk).astype(BF16), mk_t)
        m = jnp.max(s, axis=-1, keepdims=True)
        p = jnp.exp(s - m)
        l = jnp.sum(p, axis=-1, keepdims=True)
        t = _dot_nt(p.astype(BF16), mv_t) * (1.0 / l) * msk
        out = t if out is None else out + t
    return out


def _block_diag_mean():
    i = np.arange(256)
    return jnp.asarray((i[:, None] // GROUP == i[None, :] // GROUP).astype(np.float32) / GROUP, BF16)


def _mem_head_mask():
    i = np.arange(MEM_HEADS * GROUP)
    return jnp.asarray((i[None, :] // GROUP == np.arange(MEM_HEADS)[:, None]).astype(np.float32))


def _sub_head_mask():
    i = np.arange(2 * GROUP)
    return jnp.asarray((i[None, :] // GROUP == np.arange(2)[:, None]).astype(np.float32))


def _bucket_np(n):
    n = np.maximum(n, 0)
    max_exact = N_BUCKETS // 2
    ratio = np.log(np.maximum(n, 1).astype(np.float32) / np.float32(max_exact)) / np.float32(
        math.log(MAX_DISTANCE / max_exact))
    large = max_exact + (ratio * np.float32(N_BUCKETS - max_exact)).astype(np.int32)
    large = np.minimum(large, N_BUCKETS - 1)
    return np.where(n < max_exact, n, large).astype(np.int32)


def _mem_kv_kernel(mem_ref, g_ref, w_ref, gk_ref, bd_ref, k_ref, v_ref):
    xn = _rms(mem_ref[0], g_ref[0]).astype(BF16)
    z = _dot(xn, w_ref[0])
    w = z.shape[-1] // 2
    k_ref[0, 0] = _group_rms(z[:, :w], bd_ref[...], gk_ref[0]).T
    v_ref[0, 0] = z[:, w:].T


def _mem_kv(mem, g_mem, w_mem_kv, g_kn, bd):
    b, nm, d = mem.shape
    depth = g_mem.shape[0]
    w = w_mem_kv.shape[-1] // 2
    out = jax.ShapeDtypeStruct((depth, b, w, nm), F32)
    return pl.pallas_call(
        _mem_kv_kernel,
        grid=(depth, b),
        in_specs=[
            pl.BlockSpec((1, nm, d), lambda l, i: (i, 0, 0)),
            pl.BlockSpec((1, 1, d), lambda l, i: (l, 0, 0)),
            pl.BlockSpec((1, d, 2 * w), lambda l, i: (l, 0, 0)),
            pl.BlockSpec((1, 1, w), lambda l, i: (l, 0, 0)),
            pl.BlockSpec((256, 256), lambda l, i: (0, 0)),
        ],
        out_specs=[pl.BlockSpec((1, 1, w, nm), lambda l, i: (l, i, 0, 0))] * 2,
        out_shape=[out, out],
        compiler_params=_cparams(2),
        name="mem_kv",
    )(mem, g_mem.reshape(depth, 1, d), w_mem_kv.astype(BF16),
      jnp.tile(g_kn, (1, MEM_HEADS)).reshape(depth, 1, w), bd)


def _layer_a_kernel(h_ref, prev_ref, mk_ref, mv_ref, gmix_ref, win_ref, wconv_ref, wout_ref, gq_ref,
                    bd_ref, hmask_ref, out_ref, conv_ref, ubuf):
    j = pl.program_id(1)
    ts = h_ref.shape[1]
    tw = wconv_ref.shape[1]
    h = h_ref[0]
    xn = _rms(h, gmix_ref[...]).astype(BF16)
    z = _dot(xn, win_ref[...])
    u = z[:, tw:2 * tw] * z[:, 2 * tw:3 * tw]

    @pl.when(j == 0)
    def _():
        ubuf[6:8, :] = prev_ref[0]

    @pl.when(j > 0)
    def _():
        ubuf[6:8, :] = ubuf[ts + 6:ts + 8, :]

    ubuf[8:8 + ts, :] = u
    conv = (wconv_ref[0:1, :] * ubuf[6:6 + ts, :] + wconv_ref[1:2, :] * ubuf[7:7 + ts, :]
            + wconv_ref[2:3, :] * u)
    conv_ref[0] = ubuf[ts + 6:ts + 8, :]
    tok = z[:, :tw] * conv
    mo = _mem_attn(z[:, 3 * tw:], mk_ref[0].astype(BF16), mv_ref[0].astype(BF16), bd_ref[...],
                   gq_ref[...], hmask_ref[...])
    out_ref[0] = (h + _dot(tok.astype(BF16), wout_ref[0:tw, :])
                  + _dot(mo.astype(BF16), wout_ref[tw:, :]))


def _layer_a(h, conv_prev, mem_k, mem_v, g_mix, w_in, w_conv, w_out, g_qn, bd, hmask, ts):
    b, s, d = h.shape
    tw = w_conv.shape[1]
    mw = mem_k.shape[1]
    nm = mem_k.shape[2]
    const = lambda i, j: (0, 0)
    return pl.pallas_call(
        _layer_a_kernel,
        grid=(b, s // ts),
        in_specs=[
            pl.BlockSpec((1, ts, d), lambda i, j: (i, j, 0)),
            pl.BlockSpec((1, 2, tw), lambda i, j: (i, 0, 0)),
            pl.BlockSpec((1, mw, nm), lambda i, j: (i, 0, 0)),
            pl.BlockSpec((1, mw, nm), lambda i, j: (i, 0, 0)),
            pl.BlockSpec((1, d), const),
            pl.BlockSpec(w_in.shape, const),
            pl.BlockSpec(w_conv.shape, const),
            pl.BlockSpec(w_out.shape, const),
            pl.BlockSpec((1, mw), const),
            pl.BlockSpec((256, 256), const),
            pl.BlockSpec(hmask.shape, const),
        ],
        out_specs=[pl.BlockSpec((1, ts, d), lambda i, j: (i, j, 0)),
                   pl.BlockSpec((1, 2, tw), lambda i, j: (i, 0, 0))],
        out_shape=[jax.ShapeDtypeStruct((b, s, d), F32), jax.ShapeDtypeStruct((b, 2, tw), F32)],
        scratch_shapes=[pltpu.VMEM((ts + 8, tw), F32)],
        compiler_params=_cparams(2),
        name="layer_a",
    )(h, conv_prev, mem_k, mem_v, g_mix.reshape(1, d), w_in.astype(BF16), w_conv, w_out.astype(BF16),
      jnp.tile(g_qn, MEM_HEADS).reshape(1, mw), bd, hmask)


def _sort16_network():
    def merge(lo, hi, r):
        step = 2 * r
        if step < hi - lo:
            yield from merge(lo, hi, step)
            yield from merge(lo + r, hi, step)
            yield from ((i, i + r) for i in range(lo + r, hi - r, step))
        else:
            yield (lo, lo + r)

    def sort(lo, hi):
        if hi - lo >= 1:
            mid = lo + (hi - lo) // 2
            yield from sort(lo, mid)
            yield from sort(mid + 1, hi)
            yield from merge(lo, hi, 1)

    return tuple(sort(0, PEER_TOPK - 1))


_SORT16 = _sort16_network()


def _top16_sorted(v):
    x = [v[8 * k:8 * k + 8] for k in range(PEER_KEYS // 8)]

    def exchange(i, j):
        x[i], x[j] = jnp.maximum(x[i], x[j]), jnp.minimum(x[i], x[j])

    for i, j in _SORT16:
        exchange(i, j)
    for shift in (4, 2, 1):
        other = [pltpu.roll(xk, shift, axis=0) for xk in x]
        x = [jnp.maximum(x[k], other[PEER_TOPK - 1 - k]) for k in range(PEER_TOPK)]
        for dist in (8, 4, 2, 1):
            for k in range(PEER_TOPK):
                if k & dist == 0:
                    exchange(k, k + dist)
    return x


def _pack_rows(rows):
    sid = lax.broadcasted_iota(jnp.int32, (8, LANES), 0)
    out = rows[7]
    for r in range(6, -1, -1):
        out = jnp.where(sid == r, rows[r], out)
    return out


def _tile_rows(x, n):
    return jnp.concatenate([x] * (n // x.shape[0]), axis=0)


def _peer_pre_kernel(h_ref, g_ref, wq_ref, sk_ref, hnT_ref, k1_ref, e1_ref, kap_ref, e0_ref, sc_scr):
    tb = h_ref.shape[0]
    ncol = tb // LANES
    hn = _rms(h_ref[...], g_ref[...])
    hnT_ref[...] = hn.T.astype(BF16)
    q = _dot(hn.astype(BF16), wq_ref[...]).astype(BF16)
    for hc in range(2 * PEER_HEADS):
        sc_t = _dot_nt(sk_ref[hc % 2], q[:, hc * LANES:(hc + 1) * LANES])
        for col in range(ncol):
            sc_scr[hc, col] = sc_t[:, col * LANES:(col + 1) * LANES]

    def body(idx, carry):
        hd = idx // ncol
        col = idx % ncol
        s0 = sc_scr[2 * hd, col]
        s1 = sc_scr[2 * hd + 1, col]
        a = _top16_sorted(s0)
        b = _top16_sorted(s1)
        a_lo, a_hi = _pack_rows(a[0:8]), _pack_rows(a[8:16])
        b_lo, b_hi = _pack_rows(b[0:8]), _pack_rows(b[8:16])
        pad = jnp.full((PEER_KEYS - 10 * 8, LANES), -jnp.inf, F32)
        cand = jnp.concatenate([a[0] + b_lo, a[0] + b_hi] + [a[r] + b_lo for r in range(1, 8)] + [a_hi + b[0], pad],
                               axis=0)
        best = _top16_sorted(cand)
        t = best[PEER_TOPK - 1]
        zsum = None
        for s in best:
            e = jnp.exp(s - best[0])
            zsum = e if zsum is None else zsum + e
        zsum = _tile_rows(zsum, PEER_KEYS)
        kap = jnp.full(s0.shape, PEER_TOPK + 1.0, F32)
        for r in range(PEER_TOPK - 1, -1, -1):
            kap_r = jnp.full((8, LANES), PEER_TOPK + 1.0, F32)
            for c in range(PEER_TOPK // (r + 1)):
                kap_r = jnp.where(a[r] + b[c] >= t, float(PEER_TOPK - c), kap_r)
            kap = jnp.where(s0 >= _tile_rows(a[r], PEER_KEYS), _tile_rows(kap_r, PEER_KEYS), kap)
        k1 = jnp.zeros(s1.shape, F32)
        for c in range(PEER_TOPK - 1, -1, -1):
            k1 = jnp.where(s1 >= _tile_rows(b[c], PEER_KEYS), float(PEER_TOPK - c), k1)
        k1_ref[hd, col] = pltpu.bitcast(k1.astype(BF16), jnp.uint32)
        e1 = jnp.exp(s1 - _tile_rows(b[0], PEER_KEYS)) * (1.0 / zsum)
        e1_ref[hd, col] = pltpu.bitcast(e1.astype(BF16), jnp.uint32)
        kap_ref[hd, col] = kap
        e0_ref[hd, col] = jnp.exp(s0 - _tile_rows(a[0], PEER_KEYS))
        return carry

    lax.fori_loop(0, PEER_HEADS * ncol, body, 0, unroll=2)


def _packed_rows():
    return PEER_KEYS * jnp.dtype(BF16).itemsize // 4


def _peer_pre(h2d, g_ffn, w_q, sub_keys, tb):
    n, d = h2d.shape
    ncols = n // LANES
    ncol = tb // LANES
    const2 = lambda i: (0, 0)
    prow = _packed_rows()
    tiles = ([jax.ShapeDtypeStruct((PEER_HEADS, ncols, prow, LANES), jnp.uint32)] * 2
             + [jax.ShapeDtypeStruct((PEER_HEADS, ncols, PEER_KEYS, LANES), F32)] * 2)
    packed_spec = pl.BlockSpec((PEER_HEADS, ncol, prow, LANES), lambda i: (0, i, 0, 0))
    tile_spec = pl.BlockSpec((PEER_HEADS, ncol, PEER_KEYS, LANES), lambda i: (0, i, 0, 0))
    return pl.pallas_call(
        _peer_pre_kernel,
        grid=(n // tb,),
        in_specs=[
            pl.BlockSpec((tb, d), lambda i: (i, 0)),
            pl.BlockSpec((1, d), const2),
            pl.BlockSpec(w_q.shape, const2),
            pl.BlockSpec(sub_keys.shape, lambda i: (0, 0, 0)),
        ],
        out_specs=[pl.BlockSpec((d, tb), lambda i: (0, i))] + [packed_spec] * 2 + [tile_spec] * 2,
        out_shape=[jax.ShapeDtypeStruct((d, n), BF16)] + tiles,
        scratch_shapes=[pltpu.VMEM((2 * PEER_HEADS, ncol, PEER_KEYS, LANES), F32)],
        compiler_params=_cparams(1),
        name="peer_pre",
    )(h2d, g_ffn.reshape(1, d), w_q.astype(BF16), sub_keys.astype(BF16))


GATE_ROW_SPLIT = 2


def _row_tile_bf16(ref, hd, col, i, nrow):
    row = jnp.broadcast_to(ref[hd, col, pl.ds(i, 1), :], (16, LANES)).astype(BF16)
    return jnp.concatenate([row] * (nrow // 16), axis=0)


def _gate_slab(a_ref, p_ref, sub, i, k1_ref, e1_ref, kap_ref, e0_ref):
    sqrt_half = np.float32(np.sqrt(0.5))
    nrow = PEER_KEYS // GATE_ROW_SPLIT
    wrow = nrow * _packed_rows() // PEER_KEYS
    for col in range(a_ref.shape[2] // LANES):
        cs = slice(col * LANES, (col + 1) * LANES)
        kaps = [_row_tile_bf16(kap_ref, hd, col, i, nrow) for hd in range(PEER_HEADS)]
        e0s = [_row_tile_bf16(e0_ref, hd, col, i, nrow) for hd in range(PEER_HEADS)]
        for part in range(GATE_ROW_SPLIT):
            w = None
            for hd in range(PEER_HEADS):
                kap, e0 = kaps[hd], e0s[hd]
                k1 = pltpu.bitcast(k1_ref[hd, col, part * wrow:(part + 1) * wrow, :], BF16)
                e1 = pltpu.bitcast(e1_ref[hd, col, part * wrow:(part + 1) * wrow, :], BF16)
                t = jnp.where(k1 >= kap, e1, jnp.zeros((), BF16)) * e0
                w = t if w is None else w + t
            a = a_ref[sub, part * nrow:(part + 1) * nrow, cs]
            act = 0.5 * a * (1.0 + lax.erf(a * sqrt_half))
            p_ref[sub, part * nrow:(part + 1) * nrow, cs] = w * act.astype(BF16)


def _peer_dense_kernel(hnT_ref, k1_ref, e1_ref, kap_ref, e0_ref, u_ref, vt_ref, h_ref, out_ref, acc, a_scr, p_scr):
    k = pl.program_id(1)
    nsub = a_scr.shape[0]

    @pl.when(k == 0)
    def _():
        acc[...] = jnp.zeros_like(acc)

    a_scr[...] = _dot(u_ref[...], hnT_ref[...]).reshape(a_scr.shape)

    def body(sub, carry):
        _gate_slab(a_scr, p_scr, sub, k * nsub + sub, k1_ref, e1_ref, kap_ref, e0_ref)
        return carry

    lax.fori_loop(0, nsub, body, 0)
    acc[...] += _dot(vt_ref[...], p_scr[...].reshape(nsub * LANES, p_scr.shape[2]))

    @pl.when(k == pl.num_programs(1) - 1)
    def _():
        out_ref[...] = h_ref[...] + acc[...].T


def _peer_dense(h2d, hnT, k1, e1, kap, e0, u_bf, vt_bf, layer, tb, ec):
    n, d = h2d.shape
    ne = u_bf.shape[1]
    ncol = tb // LANES
    packed_spec = pl.BlockSpec((PEER_HEADS, ncol, _packed_rows(), LANES), lambda t, k: (0, t, 0, 0))
    tile_spec = pl.BlockSpec((PEER_HEADS, ncol, PEER_KEYS, LANES), lambda t, k: (0, t, 0, 0))
    return pl.pallas_call(
        _peer_dense_kernel,
        grid=(n // tb, ne // ec),
        in_specs=[pl.BlockSpec((d, tb), lambda t, k: (0, t))] + [packed_spec] * 2 + [tile_spec] * 2 + [
            pl.BlockSpec((None, ec, d), lambda t, k: (layer, k, 0)),
            pl.BlockSpec((None, d, ec), lambda t, k: (layer, 0, k)),
            pl.BlockSpec((tb, d), lambda t, k: (t, 0)),
        ],
        out_specs=pl.BlockSpec((tb, d), lambda t, k: (t, 0)),
        out_shape=jax.ShapeDtypeStruct((n, d), F32),
        scratch_shapes=[pltpu.VMEM((d, tb), F32),
                        pltpu.VMEM((ec // LANES, LANES, tb), F32),
                        pltpu.VMEM((ec // LANES, LANES, tb), BF16)],
        compiler_params=_cparams(2),
        name="peer_dense",
    )(hnT, k1, e1, kap, e0, u_bf, vt_bf, h2d)


def _peer_ffn(h2d, g_ffn, w_q, sub_keys, u_bf, vt_bf, layer, tb, ec):
    hnT, k1, e1, kap, e0 = _peer_pre(h2d, g_ffn, w_q, sub_keys, tb)
    return _peer_dense(h2d, hnT, k1, e1, kap, e0, u_bf, vt_bf, layer, tb, ec)


def _shared_kv_kernel(head_major, h_ref, g_ref, w_ref, gk_ref, bd_ref, k_ref, v_ref, *bf_refs):
    tw = gk_ref.shape[-1]
    xn = _rms(h_ref[0], g_ref[...]).astype(BF16)
    z = _dot(xn, w_ref[...])
    k = _group_rms(z[:, :tw], bd_ref[...], gk_ref[...])
    v = z[:, tw:]
    if not head_major:
        k_ref[0] = k
        v_ref[0] = v
        return
    kt_ref, vh_ref = bf_refs
    k_t = k.T
    k_ref[0] = k_t
    for hd in range(DIFF_HEADS):
        sl = slice(hd * LANES, (hd + 1) * LANES)
        kt_ref[0, hd] = k_t[sl, :].astype(BF16)
        v_ref[0, hd] = v[:, sl]
        vh_ref[0, hd] = v[:, sl].astype(BF16)


def _shared_kv(h, g_kv, w_kv, g_kn, bd, ts, head_major):
    b, s, d = h.shape
    tw = DIFF_HEADS * 2 * GROUP
    const = lambda i, j: (0, 0)
    if head_major:
        out_specs = [pl.BlockSpec((1, tw, ts), lambda i, j: (i, 0, j)),
                     pl.BlockSpec((1, DIFF_HEADS, ts, LANES), lambda i, j: (i, 0, j, 0)),
                     pl.BlockSpec((1, DIFF_HEADS, LANES, ts), lambda i, j: (i, 0, 0, j)),
                     pl.BlockSpec((1, DIFF_HEADS, ts, LANES), lambda i, j: (i, 0, j, 0))]
        out_shape = [jax.ShapeDtypeStruct((b, tw, s), F32),
                     jax.ShapeDtypeStruct((b, DIFF_HEADS, s, LANES), F32),
                     jax.ShapeDtypeStruct((b, DIFF_HEADS, LANES, s), BF16),
                     jax.ShapeDtypeStruct((b, DIFF_HEADS, s, LANES), BF16)]
    else:
        out_specs = [pl.BlockSpec((1, ts, tw), lambda i, j: (i, j, 0))] * 2
        out_shape = [jax.ShapeDtypeStruct((b, s, tw), F32)] * 2
    return pl.pallas_call(
        functools.partial(_shared_kv_kernel, head_major),
        grid=(b, s // ts),
        in_specs=[
            pl.BlockSpec((1, ts, d), lambda i, j: (i, j, 0)),
            pl.BlockSpec((1, d), const),
            pl.BlockSpec(w_kv.shape, const),
            pl.BlockSpec((1, tw), const),
            pl.BlockSpec((256, 256), const),
        ],
        out_specs=out_specs,
        out_shape=out_shape,
        compiler_params=_cparams(2),
        name="shared_kv",
    )(h, g_kv.reshape(1, d), w_kv.astype(BF16), jnp.tile(g_kn, 2 * DIFF_HEADS).reshape(1, tw), bd)


def _rel_bias_kernel(rb_ref, bucket_ref, out_ref):
    hd = pl.program_id(0)
    bk = bucket_ref[0]
    far = rb_ref[N_BUCKETS - 1, hd]
    acc = jnp.zeros(bk.shape, F32)
    for n in range(N_BUCKETS - 1):
        acc = jnp.where(bk == n, rb_ref[n, hd] - far, acc)
    out_ref[0, 0] = jnp.where(bk < 0, NEG, acc)


def _rel_bias_tiles(rel_bias, buckets):
    kinds, r, c = buckets.shape
    return pl.pallas_call(
        _rel_bias_kernel,
        grid=(DIFF_HEADS, kinds),
        in_specs=[pl.BlockSpec(memory_space=pltpu.SMEM),
                  pl.BlockSpec((1, r, c), lambda h, k: (k, 0, 0))],
        out_specs=pl.BlockSpec((1, 1, r, c), lambda h, k: (h, k, 0, 0)),
        out_shape=jax.ShapeDtypeStruct((DIFF_HEADS, kinds, r, c), F32),
        compiler_params=_cparams(2),
        name="rel_bias_tiles",
    )(rel_bias, jnp.asarray(buckets))


def _lambda_value(lam_ref, lam_init):
    lp = lam_ref[...]
    s01 = jnp.sum(lp[0:1] * lp[1:2], axis=-1, keepdims=True)
    s23 = jnp.sum(lp[2:3] * lp[3:4], axis=-1, keepdims=True)
    return jnp.exp(s01) - jnp.exp(s23) + lam_init


def _layer_b_prompt_kernel(lam_init, h_ref, kt_ref, vh_ref, mk_ref, mv_ref, gmix_ref, win_ref, gq_ref,
                           lam_ref, gsub_ref, wout_ref, gmq_ref, bd_ref, hmask_ref, cmask_ref, bias_ref,
                           out_ref, q_scr, o_scr):
    qb = pl.program_id(1)
    t = h_ref.shape[1]
    tw = DIFF_HEADS * LANES
    h = h_ref[0]
    xn = _rms(h, gmix_ref[...]).astype(BF16)
    z = _dot(xn, win_ref[...])
    q = _group_rms(z[:, :tw], bd_ref[...], gq_ref[...]) * (GROUP ** -0.5)
    for hd in range(DIFF_HEADS):
        qh = q[:, hd * LANES:(hd + 1) * LANES]
        for c in range(2):
            q_scr[2 * hd + c] = (qh * cmask_ref[c:c + 1, :]).astype(BF16)
    lam = _lambda_value(lam_ref, lam_init)

    def attend(nk):
        w = nk * t

        def head_body(hd, carry):
            outs = []
            for c in range(2):
                s = _dot(q_scr[2 * hd + c], kt_ref[0, hd, :, 0:w])
                near = [s[:, w - t:] + bias_ref[hd, 0]]
                if nk > 1:
                    near = [s[:, w - 2 * t:w - t] + bias_ref[hd, 1]] + near
                if nk > 2:
                    near = [s[:, :w - 2 * t]] + near
                s = near[0] if len(near) == 1 else jnp.concatenate(near, axis=-1)
                p = jnp.exp(s - jnp.max(s, axis=-1, keepdims=True))
                l = jnp.sum(p, axis=-1, keepdims=True)
                outs.append(_dot(p.astype(BF16), vh_ref[0, hd, 0:w, :]) * (1.0 / l))
            o = outs[0] - lam * outs[1]
            o_scr[hd] = _rms(o, gsub_ref[...]) * (1.0 - lam_init)
            return carry

        lax.fori_loop(0, DIFF_HEADS, head_body, 0)

    for nk in range(1, kt_ref.shape[3] // t + 1):
        pl.when(qb == nk - 1)(functools.partial(attend, nk))

    tok = jnp.concatenate([o_scr[hd] for hd in range(DIFF_HEADS)], axis=-1)
    mo = _mem_attn(z[:, tw:], mk_ref[0].astype(BF16), mv_ref[0].astype(BF16), bd_ref[...],
                   gmq_ref[...], hmask_ref[...])
    out_ref[0] = (h + _dot(tok.astype(BF16), wout_ref[0:tw, :])
                  + _dot(mo.astype(BF16), wout_ref[tw:, :]))


def _layer_b_prompt(h, kt, vh, mem_k, mem_v, g_mix, w_in, g_qn, lam_b, g_subln, w_out, g_mqn, bd, hmask,
                    cmask, bias, lam_init, t):
    b, s, d = h.shape
    tw = DIFF_HEADS * LANES
    mw = mem_k.shape[1]
    nm = mem_k.shape[2]
    const = lambda i, j: (0, 0)
    return pl.pallas_call(
        functools.partial(_layer_b_prompt_kernel, lam_init),
        grid=(b, s // t),
        in_specs=[
            pl.BlockSpec((1, t, d), lambda i, j: (i, j, 0)),
            pl.BlockSpec((1,) + kt.shape[1:], lambda i, j: (i, 0, 0, 0)),
            pl.BlockSpec((1,) + vh.shape[1:], lambda i, j: (i, 0, 0, 0)),
            pl.BlockSpec((1, mw, nm), lambda i, j: (i, 0, 0)),
            pl.BlockSpec((1, mw, nm), lambda i, j: (i, 0, 0)),
            pl.BlockSpec((1, d), const),
            pl.BlockSpec(w_in.shape, const),
            pl.BlockSpec((1, tw), const),
            pl.BlockSpec(lam_b.shape, const),
            pl.BlockSpec((1, LANES), const),
            pl.BlockSpec(w_out.shape, const),
            pl.BlockSpec((1, mw), const),
            pl.BlockSpec((256, 256), const),
            pl.BlockSpec(hmask.shape, const),
            pl.BlockSpec(cmask.shape, const),
            pl.BlockSpec(bias.shape, lambda i, j: (0, 0, 0, 0)),
        ],
        out_specs=pl.BlockSpec((1, t, d), lambda i, j: (i, j, 0)),
        out_shape=jax.ShapeDtypeStruct((b, s, d), F32),
        scratch_shapes=[pltpu.VMEM((2 * DIFF_HEADS, t, LANES), BF16),
                        pltpu.VMEM((DIFF_HEADS, t, LANES), F32)],
        compiler_params=_cparams(2),
        name="layer_b_prompt",
    )(h, kt, vh, mem_k, mem_v, g_mix.reshape(1, d), w_in.astype(BF16),
      jnp.tile(g_qn, 2 * DIFF_HEADS).reshape(1, tw), lam_b, g_subln.reshape(1, LANES),
      w_out.astype(BF16), jnp.tile(g_mqn, MEM_HEADS).reshape(1, mw), bd, hmask, cmask, bias)


def _layer_b_sample_kernel(lam_init, npg, pt_ref, h_ref, *refs):
    del pt_ref
    ck_refs, cv_refs = refs[:npg], refs[npg:2 * npg]
    (kn_ref, vn_ref, mk_ref, mv_ref, gmix_ref, win_ref, gq_ref, lam_ref, gsub_ref, wout_ref, gmq_ref, bd_ref,
     hmask_ref, qmask_ref, bias_ref, bias_new_ref, out_ref, qbd, qm_scr, m_scr, l_scr, acc, kv_scr) = refs[2 * npg:]
    p = pl.program_id(1)
    last = pl.num_programs(1) - 1
    sq = h_ref.shape[1]
    tw = DIFF_HEADS * LANES
    rows = qbd.shape[0]

    @pl.when(p == 0)
    def _():
        xn = _rms(h_ref[0], gmix_ref[...]).astype(BF16)
        z = _dot(xn, win_ref[...])
        q = _group_rms(z[:, :tw], bd_ref[...], gq_ref[...]) * (GROUP ** -0.5)
        qbd[...] = (jnp.concatenate([q] * (rows // sq), axis=0) * qmask_ref[...]).astype(BF16)
        qm_scr[...] = z[:, tw:]
        m_scr[...] = jnp.full(m_scr.shape, NEG, F32)
        l_scr[...] = jnp.zeros_like(l_scr)
        acc[...] = jnp.zeros_like(acc)

    def update(s, v_heads):
        m = m_scr[...]
        m_new = jnp.maximum(m, jnp.max(s, axis=-1, keepdims=True))
        alpha = jnp.exp(m - m_new)
        pr = jnp.exp(s - m_new)
        l_scr[...] = alpha * l_scr[...] + jnp.sum(pr, axis=-1, keepdims=True)
        pr = pr.astype(BF16)
        pv = [_dot(pr[hd * 2 * sq:(hd + 1) * 2 * sq], v_heads[hd]) for hd in range(DIFF_HEADS)]
        acc[...] = alpha * acc[...] + jnp.concatenate(pv, axis=0)
        m_scr[...] = m_new

    kind = jnp.where(p == last, 1, 0)
    kt = jnp.concatenate([ck_refs[t][0].astype(BF16) for t in range(npg)], axis=1)
    update(_dot(qbd[...], kt) + bias_ref[kind],
           [jnp.concatenate([cv_refs[t][0, hd].astype(BF16) for t in range(npg)], axis=0)
            for hd in range(DIFF_HEADS)])

    @pl.when(p == last)
    def _():
        kv_scr[...] = jnp.zeros_like(kv_scr)
        kv_scr[0, 0:sq, :] = kn_ref[0]
        kv_scr[1, 0:sq, :] = vn_ref[0]
        v_new = kv_scr[1].astype(BF16)
        update(_dot_nt(qbd[...], kv_scr[0].astype(BF16)) + bias_new_ref[...],
               [v_new[:, hd * LANES:(hd + 1) * LANES] for hd in range(DIFF_HEADS)])
        lam = _lambda_value(lam_ref, lam_init)
        o_all = acc[...] * (1.0 / l_scr[...])
        toks = []
        for hd in range(DIFF_HEADS):
            r0 = hd * 2 * sq
            o = o_all[r0:r0 + sq] - lam * o_all[r0 + sq:r0 + 2 * sq]
            toks.append(_rms(o, gsub_ref[...]) * (1.0 - lam_init))
        tok = jnp.concatenate(toks, axis=-1)
        mo = _mem_attn(qm_scr[...], mk_ref[0].astype(BF16), mv_ref[0].astype(BF16), bd_ref[...],
                       gmq_ref[...], hmask_ref[...])
        out_ref[0] = (h_ref[0] + _dot(tok.astype(BF16), wout_ref[0:tw, :])
                      + _dot(mo.astype(BF16), wout_ref[tw:, :]))


def _layer_b_sample(h, cache_kt, cache_vh, page_table, k_new, v_new, mem_k, mem_v, g_mix, w_in, g_qn, lam_b,
                    g_subln, w_out, g_mqn, bd, hmask, qmask, bias, bias_new, lam_init, npg):
    b, sq, d = h.shape
    n_pages = page_table.shape[1]
    page = cache_kt.shape[2]
    tw = DIFF_HEADS * LANES
    mw = mem_k.shape[1]
    nm = mem_k.shape[2]
    rows = 2 * DIFF_HEADS * sq
    const = lambda i, j, pt: (0, 0)
    per_b = lambda i, j, pt: (i, 0, 0)

    def page_map(t, nd):
        return lambda i, j, pt: (pt[i, j * npg + t],) + (0,) * nd

    grid_spec = pltpu.PrefetchScalarGridSpec(
        num_scalar_prefetch=1,
        grid=(b, n_pages // npg),
        in_specs=[pl.BlockSpec((1, sq, d), per_b)]
        + [pl.BlockSpec((1, tw, page), page_map(t, 2)) for t in range(npg)]
        + [pl.BlockSpec((1, DIFF_HEADS, page, LANES), page_map(t, 3)) for t in range(npg)]
        + [
            pl.BlockSpec((1, sq, tw), per_b),
            pl.BlockSpec((1, sq, tw), per_b),
            pl.BlockSpec((1, mw, nm), per_b),
            pl.BlockSpec((1, mw, nm), per_b),
            pl.BlockSpec((1, d), const),
            pl.BlockSpec(w_in.shape, const),
            pl.BlockSpec((1, tw), const),
            pl.BlockSpec(lam_b.shape, const),
            pl.BlockSpec((1, LANES), const),
            pl.BlockSpec(w_out.shape, const),
            pl.BlockSpec((1, mw), const),
            pl.BlockSpec((256, 256), const),
            pl.BlockSpec(hmask.shape, const),
            pl.BlockSpec(qmask.shape, const),
            pl.BlockSpec(bias.shape, lambda i, j, pt: (0, 0, 0)),
            pl.BlockSpec(bias_new.shape, const),
        ],
        out_specs=pl.BlockSpec((1, sq, d), per_b),
        scratch_shapes=[pltpu.VMEM((rows, tw), BF16),
                        pltpu.VMEM((sq, mw), F32),
                        pltpu.VMEM((rows, 1), F32),
                        pltpu.VMEM((rows, 1), F32),
                        pltpu.VMEM((rows, LANES), F32),
                        pltpu.VMEM((2, page, tw), F32)],
    )
    return pl.pallas_call(
        functools.partial(_layer_b_sample_kernel, lam_init, npg),
        grid_spec=grid_spec,
        out_shape=jax.ShapeDtypeStruct((b, sq, d), F32),
        compiler_params=_cparams(2),
        name="layer_b_sample",
    )(page_table, h, *([cache_kt] * npg), *([cache_vh] * npg), k_new, v_new, mem_k, mem_v,
      g_mix.reshape(1, d), w_in.astype(BF16), jnp.tile(g_qn, 2 * DIFF_HEADS).reshape(1, tw), lam_b,
      g_subln.reshape(1, LANES), w_out.astype(BF16), jnp.tile(g_mqn, MEM_HEADS).reshape(1, mw), bd, hmask,
      qmask, bias, bias_new)


def _prompt_buckets(t):
    r = np.arange(t)[:, None]
    c = np.arange(t)[None, :]
    diag = np.where(c <= r, _bucket_np(r - c), -1)
    sub = _bucket_np(t + r - c)
    return np.stack([diag, sub]).astype(np.int32)


def _sample_buckets(sq, page):
    qi = np.tile(np.arange(sq), 2)[:, None]
    tok = np.arange(page)[None, :]
    far = np.full((2 * sq, page), N_BUCKETS - 1, np.int32)
    last = _bucket_np(page + qi - tok)
    new = np.where(tok <= qi, _bucket_np(qi - tok), -1)
    return np.stack([far, last, new]).astype(np.int32)


def _sample_qmask(sq):
    rows = np.arange(2 * DIFF_HEADS * sq)[:, None] // sq
    lanes = np.arange(DIFF_HEADS * LANES)[None, :] // GROUP
    return jnp.asarray((rows == lanes).astype(np.float32))


def _block(n, pref):
    return pref if n % pref == 0 else n


def kernel(x_prompt, x_sample, state_conv, cache_k, cache_v, cache_mem_k, cache_mem_v, page_table, mem_prompt, g_mix, g_ffn, w_in_a, w_conv, w_out_a, w_in_b, g_qn_b, lam_b, g_subln, w_out_b, g_kv, w_kv, g_kn, rel_bias, g_mem, w_mem_kv, g_mem_qn, g_mem_kn, w_peer_q, peer_subkeys, peer_u, peer_v):
    depth = g_mix.shape[0]
    n_a = w_in_a.shape[0]
    assert depth == 2 and n_a == 1, "one short-conv layer followed by one differential-attention layer"
    bp, sp, d = x_prompt.shape
    bs, ss, _ = x_sample.shape
    nm = mem_prompt.shape[1]
    mw = MEM_HEADS * GROUP
    tw = DIFF_HEADS * 2 * GROUP
    page = cache_k.shape[1]
    t_attn = _block(sp, 256)

    bd = _block_diag_mean()
    hmask = _mem_head_mask()
    cmask = _sub_head_mask()
    u_bf = peer_u.astype(BF16)
    vt_bf = jnp.swapaxes(peer_v.astype(BF16), 1, 2)

    def mem_t(x):
        return jnp.transpose(x, (0, 1, 3, 4, 2)).reshape(depth, x.shape[1], mw, nm)

    def mem_t_inv(x):
        return jnp.transpose(x.reshape(depth, x.shape[1], MEM_HEADS, GROUP, nm), (0, 1, 4, 2, 3))

    cache_kt = jnp.transpose(cache_k, (0, 2, 3, 4, 1)).reshape(cache_k.shape[0], tw, page)
    cache_vh = jnp.transpose(cache_v, (0, 2, 1, 3))

    mem_k_p, mem_v_p = _mem_kv(mem_prompt, g_mem, w_mem_kv, g_mem_kn, bd)
    mem_k_s = mem_t(cache_mem_k)
    mem_v_s = mem_t(cache_mem_v)

    bias_p = _rel_bias_tiles(rel_bias, _prompt_buckets(t_attn))
    n_pages = page_table.shape[1]
    npg = next(c for c in (8, 4, 2, 1) if n_pages % c == 0)
    bias_s = jnp.swapaxes(_rel_bias_tiles(rel_bias, _sample_buckets(ss, page)), 0, 1)
    bias_s = bias_s.reshape(3, DIFF_HEADS * 2 * ss, page)
    bias_steps = jnp.stack([jnp.zeros((bias_s.shape[1], npg * page), F32),
                            jnp.pad(bias_s[1], ((0, 0), ((npg - 1) * page, 0)))])
    lam_init = 0.8 - 0.6 * math.exp(-0.3 * 1)

    def ffn(h, l, tb):
        b, s, _ = h.shape
        return _peer_ffn(h.reshape(b * s, d), g_ffn[l], w_peer_q[l], peer_subkeys[l], u_bf, vt_bf, l,
                         tb, PEER_EXPERT_CHUNK).reshape(b, s, d)

    n = bp * sp
    tb = _block(n, 512)
    zeros_conv = jnp.zeros((bp, 2, w_conv.shape[-1]), F32)
    h, conv_p = _layer_a(x_prompt, zeros_conv, mem_k_p[0], mem_v_p[0], g_mix[0], w_in_a[0], w_conv[0],
                         w_out_a[0], g_mem_qn[0], bd, hmask, ts=_block(sp, 512))
    h = ffn(h, 0, tb)
    k_t, v_h, kt_bf, vh_bf = _shared_kv(h, g_kv, w_kv, g_kn, bd, t_attn, True)
    h = _layer_b_prompt(h, kt_bf, vh_bf, mem_k_p[1], mem_v_p[1], g_mix[1], w_in_b[0], g_qn_b[0], lam_b[0],
                        g_subln[0], w_out_b[0], g_mem_qn[1], bd, hmask, cmask, bias_p, lam_init, t_attn)
    y_p = ffn(h, 1, tb)
    k_p = jnp.transpose(k_t.reshape(bp, DIFF_HEADS, 2, GROUP, sp), (0, 4, 1, 2, 3))
    v_p = jnp.transpose(v_h, (0, 2, 1, 3))

    n = bs * ss
    tb = _block(n, 512)
    h, conv_s = _layer_a(x_sample, state_conv[0], mem_k_s[0], mem_v_s[0], g_mix[0], w_in_a[0], w_conv[0],
                         w_out_a[0], g_mem_qn[0], bd, hmask, ts=_block(ss, 512))
    h = ffn(h, 0, tb)
    k_s, v_s = _shared_kv(h.reshape(1, n, d), g_kv, w_kv, g_kn, bd, n, False)
    k_s = k_s.reshape(bs, ss, tw)
    v_s = v_s.reshape(bs, ss, tw)
    h = _layer_b_sample(h, cache_kt, cache_vh, page_table, k_s, v_s, mem_k_s[1], mem_v_s[1], g_mix[1],
                        w_in_b[0], g_qn_b[0], lam_b[0], g_subln[0], w_out_b[0], g_mem_qn[1], bd, hmask,
                        _sample_qmask(ss), bias_steps, bias_s[2], lam_init, npg)
    y_s = ffn(h, 1, tb)

    return (y_p, y_s, conv_p[None], conv_s[None], k_p, v_p,
            k_s.reshape(bs, ss, DIFF_HEADS, 2, GROUP), v_s.reshape(bs, ss, DIFF_HEADS, 2 * GROUP),
            mem_t_inv(mem_k_p), mem_t_inv(mem_v_p))
```

```python
import functools
import math

import numpy as np
import jax
import jax.numpy as jnp
from jax import lax
from jax.experimental import pallas as pl
from jax.experimental.pallas import tpu as pltpu

F32 = jnp.float32
BF16 = jnp.bfloat16

EPS = 1e-6
NEG = -1e30
LANES = 128
GROUP = 64
MEM_HEADS = 4
DIFF_HEADS = 6
N_BUCKETS = 32
MAX_DISTANCE = 128
PEER_HEADS = 8
PEER_KEYS = 128
PEER_TOPK = 16
PEER_EXPERT_CHUNK = 2048
VMEM_LIMIT = 56 * 1024 * 1024


def _cparams(n_grid):
    return pltpu.CompilerParams(dimension_semantics=("arbitrary",) * n_grid,
                                vmem_limit_bytes=VMEM_LIMIT)


def _dot(a, b):
    return jnp.dot(a, b, preferred_element_type=F32)


def _dot_nt(a, b):
    return lax.dot_general(a, b, (((1,), (1,)), ((), ())), preferred_element_type=F32)


def _rms(x, g):
    ms = jnp.mean(x * x, axis=-1, keepdims=True)
    return x * lax.rsqrt(ms + EPS) * g


def _group_rms(x, bd, g):
    x2 = x * x
    hi = x2.astype(BF16)
    lo = (x2 - hi.astype(F32)).astype(BF16)
    parts = []
    for s in range(x.shape[-1] // 256):
        sl = slice(s * 256, (s + 1) * 256)
        parts.append(_dot(hi[:, sl], bd) + _dot(lo[:, sl], bd))
    ms = parts[0] if len(parts) == 1 else jnp.concatenate(parts, axis=-1)
    return x * lax.rsqrt(ms + EPS) * g


def _mem_attn(qm, mk_t, mv_t, bd, gq, hmask):
    q = _group_rms(qm, bd, gq) * (GROUP ** -0.5)
    out = None
    for hd in range(MEM_HEADS):
        msk = hmask[hd:hd + 1, :]
        s = _dot((q * msk).astype(BF16), mk_t)
        m = jnp.max(s, axis=-1, keepdims=True)
        p = jnp.exp(s - m)
        l = jnp.sum(p, axis=-1, keepdims=True)
        t = _dot_nt(p.astype(BF16), mv_t) * (1.0 / l) * msk
        out = t if out is None else out + t
    return out


def _block_diag_mean():
    i = np.arange(256)
    return jnp.asarray((i[:, None] // GROUP == i[None, :] // GROUP).astype(np.float32) / GROUP, BF16)


def _mem_head_mask():
    i = np.arange(MEM_HEADS * GROUP)
    return jnp.asarray((i[None, :] // GROUP == np.arange(MEM_HEADS)[:, None]).astype(np.float32))


def _sub_head_mask():
    i = np.arange(2 * GROUP)
    return jnp.asarray((i[None, :] // GROUP == np.arange(2)[:, None]).astype(np.float32))


def _bucket_np(n):
    n = np.maximum(n, 0)
    max_exact = N_BUCKETS // 2
    ratio = np.log(np.maximum(n, 1).astype(np.float32) / np.float32(max_exact)) / np.float32(
        math.log(MAX_DISTANCE / max_exact))
    large = max_exact + (ratio * np.float32(N_BUCKETS - max_exact)).astype(np.int32)
    large = np.minimum(large, N_BUCKETS - 1)
    return np.where(n < max_exact, n, large).astype(np.int32)


def _mem_kv_kernel(mem_ref, g_ref, w_ref, gk_ref, bd_ref, k_ref, v_ref):
    xn = _rms(mem_ref[0], g_ref[0]).astype(BF16)
    z = _dot(xn, w_ref[0])
    w = z.shape[-1] // 2
    k_ref[0, 0] = _group_rms(z[:, :w], bd_ref[...], gk_ref[0]).T
    v_ref[0, 0] = z[:, w:].T


def _mem_kv(mem, g_mem, w_mem_kv, g_kn, bd):
    b, nm, d = mem.shape
    depth = g_mem.shape[0]
    w = w_mem_kv.shape[-1] // 2
    out = jax.ShapeDtypeStruct((depth, b, w, nm), F32)
    return pl.pallas_call(
        _mem_kv_kernel,
        grid=(depth, b),
        in_specs=[
            pl.BlockSpec((1, nm, d), lambda l, i: (i, 0, 0)),
            pl.BlockSpec((1, 1, d), lambda l, i: (l, 0, 0)),
            pl.BlockSpec((1, d, 2 * w), lambda l, i: (l, 0, 0)),
            pl.BlockSpec((1, 1, w), lambda l, i: (l, 0, 0)),
            pl.BlockSpec((256, 256), lambda l, i: (0, 0)),
        ],
        out_specs=[pl.BlockSpec((1, 1, w, nm), lambda l, i: (l, i, 0, 0))] * 2,
        out_shape=[out, out],
        compiler_params=_cparams(2),
        name="mem_kv",
    )(mem, g_mem.reshape(depth, 1, d), w_mem_kv.astype(BF16),
      jnp.tile(g_kn, (1, MEM_HEADS)).reshape(depth, 1, w), bd)


def _layer_a_kernel(h_ref, prev_ref, mk_ref, mv_ref, gmix_ref, win_ref, wconv_ref, wout_ref, gq_ref,
                    bd_ref, hmask_ref, out_ref, conv_ref, ubuf):
    j = pl.program_id(1)
    ts = h_ref.shape[1]
    tw = wconv_ref.shape[1]
    h = h_ref[0]
    xn = _rms(h, gmix_ref[...]).astype(BF16)
    z = _dot(xn, win_ref[...])
    u = z[:, tw:2 * tw] * z[:, 2 * tw:3 * tw]

    @pl.when(j == 0)
    def _():
        ubuf[6:8, :] = prev_ref[0]

    @pl.when(j > 0)
    def _():
        ubuf[6:8, :] = ubuf[ts + 6:ts + 8, :]

    ubuf[8:8 + ts, :] = u
    conv = (wconv_ref[0:1, :] * ubuf[6:6 + ts, :] + wconv_ref[1:2, :] * ubuf[7:7 + ts, :]
            + wconv_ref[2:3, :] * u)
    conv_ref[0] = ubuf[ts + 6:ts + 8, :]
    tok = z[:, :tw] * conv
    mo = _mem_attn(z[:, 3 * tw:], mk_ref[0].astype(BF16), mv_ref[0].astype(BF16), bd_ref[...],
                   gq_ref[...], hmask_ref[...])
    out_ref[0] = (h + _dot(tok.astype(BF16), wout_ref[0:tw, :])
                  + _dot(mo.astype(BF16), wout_ref[tw:, :]))


def _layer_a(h, conv_prev, mem_k, mem_v, g_mix, w_in, w_conv, w_out, g_qn, bd, hmask, ts):
    b, s, d = h.shape
    tw = w_conv.shape[1]
    mw = mem_k.shape[1]
    nm = mem_k.shape[2]
    const = lambda i, j: (0, 0)
    return pl.pallas_call(
        _layer_a_kernel,
        grid=(b, s // ts),
        in_specs=[
            pl.BlockSpec((1, ts, d), lambda i, j: (i, j, 0)),
            pl.BlockSpec((1, 2, tw), lambda i, j: (i, 0, 0)),
            pl.BlockSpec((1, mw, nm), lambda i, j: (i, 0, 0)),
            pl.BlockSpec((1, mw, nm), lambda i, j: (i, 0, 0)),
            pl.BlockSpec((1, d), const),
            pl.BlockSpec(w_in.shape, const),
            pl.BlockSpec(w_conv.shape, const),
            pl.BlockSpec(w_out.shape, const),
            pl.BlockSpec((1, mw), const),
            pl.BlockSpec((256, 256), const),
            pl.BlockSpec(hmask.shape, const),
        ],
        out_specs=[pl.BlockSpec((1, ts, d), lambda i, j: (i, j, 0)),
                   pl.BlockSpec((1, 2, tw), lambda i, j: (i, 0, 0))],
        out_shape=[jax.ShapeDtypeStruct((b, s, d), F32), jax.ShapeDtypeStruct((b, 2, tw), F32)],
        scratch_shapes=[pltpu.VMEM((ts + 8, tw), F32)],
        compiler_params=_cparams(2),
        name="layer_a",
    )(h, conv_prev, mem_k, mem_v, g_mix.reshape(1, d), w_in.astype(BF16), w_conv, w_out.astype(BF16),
      jnp.tile(g_qn, MEM_HEADS).reshape(1, mw), bd, hmask)


def _sort16_network():
    def merge(lo, hi, r):
        step = 2 * r
        if step < hi - lo:
            yield from merge(lo, hi, step)
            yield from merge(lo + r, hi, step)
            yield from ((i, i + r) for i in range(lo + r, hi - r, step))
        else:
            yield (lo, lo + r)

    def sort(lo, hi):
        if hi - lo >= 1:
            mid = lo + (hi - lo) // 2
            yield from sort(lo, mid)
            yield from sort(mid + 1, hi)
            yield from merge(lo, hi, 1)

    return tuple(sort(0, PEER_TOPK - 1))


_SORT16 = _sort16_network()


def _top16_sorted(v):
    x = [v[8 * k:8 * k + 8] for k in range(PEER_KEYS // 8)]

    def exchange(i, j):
        x[i], x[j] = jnp.maximum(x[i], x[j]), jnp.minimum(x[i], x[j])

    for i, j in _SORT16:
        exchange(i, j)
    for shift in (4, 2, 1):
        other = [pltpu.roll(xk, shift, axis=0) for xk in x]
        x = [jnp.maximum(x[k], other[PEER_TOPK - 1 - k]) for k in range(PEER_TOPK)]
        for dist in (8, 4, 2, 1):
            for k in range(PEER_TOPK):
                if k & dist == 0:
                    exchange(k, k + dist)
    return x


def _pack_rows(rows):
    sid = lax.broadcasted_iota(jnp.int32, (8, LANES), 0)
    out = rows[7]
    for r in range(6, -1, -1):
        out = jnp.where(sid == r, rows[r], out)
    return out


def _tile_rows(x, n):
    return jnp.concatenate([x] * (n // x.shape[0]), axis=0)


def _peer_pre_kernel(h_ref, g_ref, wq_ref, sk_ref, hnT_ref, k1_ref, e1_ref, kap_ref, e0_ref, sc_scr):
    tb = h_ref.shape[0]
    ncol = tb // LANES
    hn = _rms(h_ref[...], g_ref[...])
    hnT_ref[...] = hn.T.astype(BF16)
    q = _dot(hn.astype(BF16), wq_ref[...]).astype(BF16)
    for hc in range(2 * PEER_HEADS):
        sc_t = _dot_nt(sk_ref[hc % 2], q[:, hc * LANES:(hc + 1) * LANES])
        for col in range(ncol):
            sc_scr[hc, col] = sc_t[:, col * LANES:(col + 1) * LANES]

    def body(idx, carry):
        hd = idx // ncol
        col = idx % ncol
        s0 = sc_scr[2 * hd, col]
        s1 = sc_scr[2 * hd + 1, col]
        a = _top16_sorted(s0)
        b = _top16_sorted(s1)
        a_lo, a_hi = _pack_rows(a[0:8]), _pack_rows(a[8:16])
        b_lo, b_hi = _pack_rows(b[0:8]), _pack_rows(b[8:16])
        pad = jnp.full((PEER_KEYS - 10 * 8, LANES), -jnp.inf, F32)
        cand = jnp.concatenate([a[0] + b_lo, a[0] + b_hi] + [a[r] + b_lo for r in range(1, 8)] + [a_hi + b[0], pad],
                               axis=0)
        best = _top16_sorted(cand)
        t = best[PEER_TOPK - 1]
        zsum = None
        for s in best:
            e = jnp.exp(s - best[0])
            zsum = e if zsum is None else zsum + e
        zsum = _tile_rows(zsum, PEER_KEYS)
        kap = jnp.full(s0.shape, PEER_TOPK + 1.0, F32)
        for r in range(PEER_TOPK - 1, -1, -1):
            kap_r = jnp.full((8, LANES), PEER_TOPK + 1.0, F32)
            for c in range(PEER_TOPK // (r + 1)):
                kap_r = jnp.where(a[r] + b[c] >= t, float(PEER_TOPK - c), kap_r)
            kap = jnp.where(s0 >= _tile_rows(a[r], PEER_KEYS), _tile_rows(kap_r, PEER_KEYS), kap)
        k1 = jnp.zeros(s1.shape, F32)
        for c in range(PEER_TOPK - 1, -1, -1):
            k1 = jnp.where(s1 >= _tile_rows(b[c], PEER_KEYS), float(PEER_TOPK - c), k1)
        k1_ref[hd, col] = pltpu.bitcast(k1.astype(BF16), jnp.uint32)
        e1 = jnp.exp(s1 - _tile_rows(b[0], PEER_KEYS)) * (1.0 / zsum)
        e1_ref[hd, col] = pltpu.bitcast(e1.astype(BF16), jnp.uint32)
        kap_ref[hd, col] = kap
        e0_ref[hd, col] = jnp.exp(s0 - _tile_rows(a[0], PEER_KEYS))
        return carry

    lax.fori_loop(0, PEER_HEADS * ncol, body, 0, unroll=2)


def _packed_rows():
    return PEER_KEYS * jnp.dtype(BF16).itemsize // 4


def _peer_pre(h2d, g_ffn, w_q, sub_keys, tb):
    n, d = h2d.shape
    ncols = n // LANES
    ncol = tb // LANES
    const2 = lambda i: (0, 0)
    prow = _packed_rows()
    tiles = ([jax.ShapeDtypeStruct((PEER_HEADS, ncols, prow, LANES), jnp.uint32)] * 2
             + [jax.ShapeDtypeStruct((PEER_HEADS, ncols, PEER_KEYS, LANES), F32)] * 2)
    packed_spec = pl.BlockSpec((PEER_HEADS, ncol, prow, LANES), lambda i: (0, i, 0, 0))
    tile_spec = pl.BlockSpec((PEER_HEADS, ncol, PEER_KEYS, LANES), lambda i: (0, i, 0, 0))
    return pl.pallas_call(
        _peer_pre_kernel,
        grid=(n // tb,),
        in_specs=[
            pl.BlockSpec((tb, d), lambda i: (i, 0)),
            pl.BlockSpec((1, d), const2),
            pl.BlockSpec(w_q.shape, const2),
            pl.BlockSpec(sub_keys.shape, lambda i: (0, 0, 0)),
        ],
        out_specs=[pl.BlockSpec((None, d, tb), lambda i: (i, 0, 0))] + [packed_spec] * 2 + [tile_spec] * 2,
        out_shape=[jax.ShapeDtypeStruct((n // tb, d, tb), BF16)] + tiles,
        scratch_shapes=[pltpu.VMEM((2 * PEER_HEADS, ncol, PEER_KEYS, LANES), F32)],
        compiler_params=_cparams(1),
        name="peer_pre",
    )(h2d, g_ffn.reshape(1, d), w_q.astype(BF16), sub_keys.astype(BF16))


GATE_ROW_SPLIT = 2


def _row_tile_bf16(ref, hd, col, i, nrow):
    row = jnp.broadcast_to(ref[hd, col, pl.ds(i, 1), :], (16, LANES)).astype(BF16)
    return jnp.concatenate([row] * (nrow // 16), axis=0)


def _gate_slab(a_ref, p_ref, sub, i, k1_ref, e1_ref, kap_ref, e0_ref):
    sqrt_half = np.float32(np.sqrt(0.5))
    nrow = PEER_KEYS // GATE_ROW_SPLIT
    wrow = nrow * _packed_rows() // PEER_KEYS
    for col in range(a_ref.shape[2] // LANES):
        cs = slice(col * LANES, (col + 1) * LANES)
        kaps = [_row_tile_bf16(kap_ref, hd, col, i, nrow) for hd in range(PEER_HEADS)]
        e0s = [_row_tile_bf16(e0_ref, hd, col, i, nrow) for hd in range(PEER_HEADS)]
        for part in range(GATE_ROW_SPLIT):
            w = None
            for hd in range(PEER_HEADS):
                kap, e0 = kaps[hd], e0s[hd]
                k1 = pltpu.bitcast(k1_ref[hd, col, part * wrow:(part + 1) * wrow, :], BF16)
                e1 = pltpu.bitcast(e1_ref[hd, col, part * wrow:(part + 1) * wrow, :], BF16)
                t = jnp.where(k1 >= kap, e1, jnp.zeros((), BF16)) * e0
                w = t if w is None else w + t
            a = a_ref[sub, part * nrow:(part + 1) * nrow, cs]
            act = 0.5 * a * (1.0 + lax.erf(a * sqrt_half))
            p_ref[sub, part * nrow:(part + 1) * nrow, cs] = w * act.astype(BF16)


def _peer_dense_kernel(hnT_ref, k1_ref, e1_ref, kap_ref, e0_ref, u_ref, vt_ref, h_ref, out_ref, acc, a_scr, p_scr):
    k = pl.program_id(1)
    nsub = a_scr.shape[0]

    @pl.when(k == 0)
    def _():
        acc[...] = jnp.zeros_like(acc)

    a_scr[...] = _dot(u_ref[...], hnT_ref[...]).reshape(a_scr.shape)

    def body(sub, carry):
        _gate_slab(a_scr, p_scr, sub, k * nsub + sub, k1_ref, e1_ref, kap_ref, e0_ref)
        return carry

    lax.fori_loop(0, nsub, body, 0)
    acc[...] += _dot(vt_ref[...], p_scr[...].reshape(nsub * LANES, p_scr.shape[2]))

    @pl.when(k == pl.num_programs(1) - 1)
    def _():
        out_ref[...] = h_ref[...] + acc[...].T


def _peer_dense(h2d, hnT, k1, e1, kap, e0, u_bf, vt_bf, layer, tb, ec):
    n, d = h2d.shape
    ne = u_bf.shape[1]
    ncol = tb // LANES
    packed_spec = pl.BlockSpec((PEER_HEADS, ncol, _packed_rows(), LANES), lambda t, k: (0, t, 0, 0))
    tile_spec = pl.BlockSpec((PEER_HEADS, ncol, PEER_KEYS, LANES), lambda t, k: (0, t, 0, 0))
    return pl.pallas_call(
        _peer_dense_kernel,
        grid=(n // tb, ne // ec),
        in_specs=[pl.BlockSpec((None, d, tb), lambda t, k: (t, 0, 0))] + [packed_spec] * 2 + [tile_spec] * 2 + [
            pl.BlockSpec((None, ec, d), lambda t, k: (layer, k, 0)),
            pl.BlockSpec((None, None, d, ec), lambda t, k: (layer, k, 0, 0)),
            pl.BlockSpec((tb, d), lambda t, k: (t, 0)),
        ],
        out_specs=pl.BlockSpec((tb, d), lambda t, k: (t, 0)),
        out_shape=jax.ShapeDtypeStruct((n, d), F32),
        scratch_shapes=[pltpu.VMEM((d, tb), F32),
                        pltpu.VMEM((ec // LANES, LANES, tb), F32),
                        pltpu.VMEM((ec // LANES, LANES, tb), BF16)],
        compiler_params=_cparams(2),
        name="peer_dense",
    )(hnT, k1, e1, kap, e0, u_bf, vt_bf, h2d)


def _peer_ffn(h2d, g_ffn, w_q, sub_keys, u_bf, vt_bf, layer, tb, ec):
    hnT, k1, e1, kap, e0 = _peer_pre(h2d, g_ffn, w_q, sub_keys, tb)
    return _peer_dense(h2d, hnT, k1, e1, kap, e0, u_bf, vt_bf, layer, tb, ec)


def _shared_kv_kernel(head_major, h_ref, g_ref, w_ref, gk_ref, bd_ref, k_ref, v_ref, *bf_refs):
    tw = gk_ref.shape[-1]
    xn = _rms(h_ref[0], g_ref[...]).astype(BF16)
    z = _dot(xn, w_ref[...])
    k = _group_rms(z[:, :tw], bd_ref[...], gk_ref[...])
    v = z[:, tw:]
    if not head_major:
        k_ref[0] = k
        v_ref[0] = v
        return
    kt_ref, vh_ref = bf_refs
    k_t = k.T
    k_ref[0] = k_t
    for hd in range(DIFF_HEADS):
        sl = slice(hd * LANES, (hd + 1) * LANES)
        kt_ref[0, hd] = k_t[sl, :].astype(BF16)
        v_ref[0, hd] = v[:, sl]
        vh_ref[0, hd] = v[:, sl].astype(BF16)


def _shared_kv(h, g_kv, w_kv, g_kn, bd, ts, head_major):
    b, s, d = h.shape
    tw = DIFF_HEADS * 2 * GROUP
    const = lambda i, j: (0, 0)
    if head_major:
        out_specs = [pl.BlockSpec((1, tw, ts), lambda i, j: (i, 0, j)),
                     pl.BlockSpec((1, DIFF_HEADS, ts, LANES), lambda i, j: (i, 0, j, 0)),
                     pl.BlockSpec((1, DIFF_HEADS, LANES, ts), lambda i, j: (i, 0, 0, j)),
                     pl.BlockSpec((1, DIFF_HEADS, ts, LANES), lambda i, j: (i, 0, j, 0))]
        out_shape = [jax.ShapeDtypeStruct((b, tw, s), F32),
                     jax.ShapeDtypeStruct((b, DIFF_HEADS, s, LANES), F32),
                     jax.ShapeDtypeStruct((b, DIFF_HEADS, LANES, s), BF16),
                     jax.ShapeDtypeStruct((b, DIFF_HEADS, s, LANES), BF16)]
    else:
        out_specs = [pl.BlockSpec((1, ts, tw), lambda i, j: (i, j, 0))] * 2
        out_shape = [jax.ShapeDtypeStruct((b, s, tw), F32)] * 2
    return pl.pallas_call(
        functools.partial(_shared_kv_kernel, head_major),
        grid=(b, s // ts),
        in_specs=[
            pl.BlockSpec((1, ts, d), lambda i, j: (i, j, 0)),
            pl.BlockSpec((1, d), const),
            pl.BlockSpec(w_kv.shape, const),
            pl.BlockSpec((1, tw), const),
            pl.BlockSpec((256, 256), const),
        ],
        out_specs=out_specs,
        out_shape=out_shape,
        compiler_params=_cparams(2),
        name="shared_kv",
    )(h, g_kv.reshape(1, d), w_kv.astype(BF16), jnp.tile(g_kn, 2 * DIFF_HEADS).reshape(1, tw), bd)


def _rel_bias_kernel(rb_ref, bucket_ref, out_ref):
    hd = pl.program_id(0)
    bk = bucket_ref[0]
    far = rb_ref[N_BUCKETS - 1, hd]
    acc = jnp.zeros(bk.shape, F32)
    for n in range(N_BUCKETS - 1):
        acc = jnp.where(bk == n, rb_ref[n, hd] - far, acc)
    out_ref[0, 0] = jnp.where(bk < 0, NEG, acc)


def _rel_bias_tiles(rel_bias, buckets):
    kinds, r, c = buckets.shape
    return pl.pallas_call(
        _rel_bias_kernel,
        grid=(DIFF_HEADS, kinds),
        in_specs=[pl.BlockSpec(memory_space=pltpu.SMEM),
                  pl.BlockSpec((1, r, c), lambda h, k: (k, 0, 0))],
        out_specs=pl.BlockSpec((1, 1, r, c), lambda h, k: (h, k, 0, 0)),
        out_shape=jax.ShapeDtypeStruct((DIFF_HEADS, kinds, r, c), F32),
        compiler_params=_cparams(2),
        name="rel_bias_tiles",
    )(rel_bias, jnp.asarray(buckets))


def _lambda_value(lam_ref, lam_init):
    lp = lam_ref[...]
    s01 = jnp.sum(lp[0:1] * lp[1:2], axis=-1, keepdims=True)
    s23 = jnp.sum(lp[2:3] * lp[3:4], axis=-1, keepdims=True)
    return jnp.exp(s01) - jnp.exp(s23) + lam_init


def _layer_b_prompt_kernel(lam_init, h_ref, kt_ref, vh_ref, mk_ref, mv_ref, gmix_ref, win_ref, gq_ref,
                           lam_ref, gsub_ref, wout_ref, gmq_ref, bd_ref, hmask_ref, cmask_ref, bias_ref,
                           out_ref, q_scr, o_scr):
    qb = pl.program_id(1)
    t = h_ref.shape[1]
    tw = DIFF_HEADS * LANES
    h = h_ref[0]
    xn = _rms(h, gmix_ref[...]).astype(BF16)
    z = _dot(xn, win_ref[...])
    q = _group_rms(z[:, :tw], bd_ref[...], gq_ref[...]) * (GROUP ** -0.5)
    for hd in range(DIFF_HEADS):
        qh = q[:, hd * LANES:(hd + 1) * LANES]
        for c in range(2):
            q_scr[2 * hd + c] = (qh * cmask_ref[c:c + 1, :]).astype(BF16)
    lam = _lambda_value(lam_ref, lam_init)

    def attend(nk):
        w = nk * t

        def head_body(hd, carry):
            outs = []
            for c in range(2):
                s = _dot(q_scr[2 * hd + c], kt_ref[0, hd, :, 0:w])
                near = [s[:, w - t:] + bias_ref[hd, 0]]
                if nk > 1:
                    near = [s[:, w - 2 * t:w - t] + bias_ref[hd, 1]] + near
                if nk > 2:
                    near = [s[:, :w - 2 * t]] + near
                s = near[0] if len(near) == 1 else jnp.concatenate(near, axis=-1)
                p = jnp.exp(s - jnp.max(s, axis=-1, keepdims=True))
                l = jnp.sum(p, axis=-1, keepdims=True)
                outs.append(_dot(p.astype(BF16), vh_ref[0, hd, 0:w, :]) * (1.0 / l))
            o = outs[0] - lam * outs[1]
            o_scr[hd] = _rms(o, gsub_ref[...]) * (1.0 - lam_init)
            return carry

        lax.fori_loop(0, DIFF_HEADS, head_body, 0)

    for nk in range(1, kt_ref.shape[3] // t + 1):
        pl.when(qb == nk - 1)(functools.partial(attend, nk))

    tok = jnp.concatenate([o_scr[hd] for hd in range(DIFF_HEADS)], axis=-1)
    mo = _mem_attn(z[:, tw:], mk_ref[0].astype(BF16), mv_ref[0].astype(BF16), bd_ref[...],
                   gmq_ref[...], hmask_ref[...])
    out_ref[0] = (h + _dot(tok.astype(BF16), wout_ref[0:tw, :])
                  + _dot(mo.astype(BF16), wout_ref[tw:, :]))


def _layer_b_prompt(h, kt, vh, mem_k, mem_v, g_mix, w_in, g_qn, lam_b, g_subln, w_out, g_mqn, bd, hmask,
                    cmask, bias, lam_init, t):
    b, s, d = h.shape
    tw = DIFF_HEADS * LANES
    mw = mem_k.shape[1]
    nm = mem_k.shape[2]
    const = lambda i, j: (0, 0)
    return pl.pallas_call(
        functools.partial(_layer_b_prompt_kernel, lam_init),
        grid=(b, s // t),
        in_specs=[
            pl.BlockSpec((1, t, d), lambda i, j: (i, j, 0)),
            pl.BlockSpec((1,) + kt.shape[1:], lambda i, j: (i, 0, 0, 0)),
            pl.BlockSpec((1,) + vh.shape[1:], lambda i, j: (i, 0, 0, 0)),
            pl.BlockSpec((1, mw, nm), lambda i, j: (i, 0, 0)),
            pl.BlockSpec((1, mw, nm), lambda i, j: (i, 0, 0)),
            pl.BlockSpec((1, d), const),
            pl.BlockSpec(w_in.shape, const),
            pl.BlockSpec((1, tw), const),
            pl.BlockSpec(lam_b.shape, const),
            pl.BlockSpec((1, LANES), const),
            pl.BlockSpec(w_out.shape, const),
            pl.BlockSpec((1, mw), const),
            pl.BlockSpec((256, 256), const),
            pl.BlockSpec(hmask.shape, const),
            pl.BlockSpec(cmask.shape, const),
            pl.BlockSpec(bias.shape, lambda i, j: (0, 0, 0, 0)),
        ],
        out_specs=pl.BlockSpec((1, t, d), lambda i, j: (i, j, 0)),
        out_shape=jax.ShapeDtypeStruct((b, s, d), F32),
        scratch_shapes=[pltpu.VMEM((2 * DIFF_HEADS, t, LANES), BF16),
                        pltpu.VMEM((DIFF_HEADS, t, LANES), F32)],
        compiler_params=_cparams(2),
        name="layer_b_prompt",
    )(h, kt, vh, mem_k, mem_v, g_mix.reshape(1, d), w_in.astype(BF16),
      jnp.tile(g_qn, 2 * DIFF_HEADS).reshape(1, tw), lam_b, g_subln.reshape(1, LANES),
      w_out.astype(BF16), jnp.tile(g_mqn, MEM_HEADS).reshape(1, mw), bd, hmask, cmask, bias)


def _layer_b_sample_kernel(lam_init, npg, pt_ref, h_ref, *refs):
    del pt_ref
    ck_refs, cv_refs = refs[:npg], refs[npg:2 * npg]
    (kn_ref, vn_ref, mk_ref, mv_ref, gmix_ref, win_ref, gq_ref, lam_ref, gsub_ref, wout_ref, gmq_ref, bd_ref,
     hmask_ref, qmask_ref, bias_ref, bias_new_ref, out_ref, qbd, qm_scr, m_scr, l_scr, acc, kv_scr) = refs[2 * npg:]
    p = pl.program_id(1)
    last = pl.num_programs(1) - 1
    sq = h_ref.shape[1]
    tw = DIFF_HEADS * LANES
    rows = qbd.shape[0]

    @pl.when(p == 0)
    def _():
        xn = _rms(h_ref[0], gmix_ref[...]).astype(BF16)
        z = _dot(xn, win_ref[...])
        q = _group_rms(z[:, :tw], bd_ref[...], gq_ref[...]) * (GROUP ** -0.5)
        qbd[...] = (jnp.concatenate([q] * (rows // sq), axis=0) * qmask_ref[...]).astype(BF16)
        qm_scr[...] = z[:, tw:]
        m_scr[...] = jnp.full(m_scr.shape, NEG, F32)
        l_scr[...] = jnp.zeros_like(l_scr)
        acc[...] = jnp.zeros_like(acc)

    def update(s, v_heads):
        m = m_scr[...]
        m_new = jnp.maximum(m, jnp.max(s, axis=-1, keepdims=True))
        alpha = jnp.exp(m - m_new)
        pr = jnp.exp(s - m_new)
        l_scr[...] = alpha * l_scr[...] + jnp.sum(pr, axis=-1, keepdims=True)
        pr = pr.astype(BF16)
        pv = [_dot(pr[hd * 2 * sq:(hd + 1) * 2 * sq], v_heads[hd]) for hd in range(DIFF_HEADS)]
        acc[...] = alpha * acc[...] + jnp.concatenate(pv, axis=0)
        m_scr[...] = m_new

    kind = jnp.where(p == last, 1, 0)
    kt = jnp.concatenate([ck_refs[t][0].astype(BF16) for t in range(npg)], axis=1)
    update(_dot(qbd[...], kt) + bias_ref[kind],
           [jnp.concatenate([cv_refs[t][0, hd].astype(BF16) for t in range(npg)], axis=0)
            for hd in range(DIFF_HEADS)])

    @pl.when(p == last)
    def _():
        kv_scr[...] = jnp.zeros_like(kv_scr)
        kv_scr[0, 0:sq, :] = kn_ref[0]
        kv_scr[1, 0:sq, :] = vn_ref[0]
        v_new = kv_scr[1].astype(BF16)
        update(_dot_nt(qbd[...], kv_scr[0].astype(BF16)) + bias_new_ref[...],
               [v_new[:, hd * LANES:(hd + 1) * LANES] for hd in range(DIFF_HEADS)])
        lam = _lambda_value(lam_ref, lam_init)
        o_all = acc[...] * (1.0 / l_scr[...])
        toks = []
        for hd in range(DIFF_HEADS):
            r0 = hd * 2 * sq
            o = o_all[r0:r0 + sq] - lam * o_all[r0 + sq:r0 + 2 * sq]
            toks.append(_rms(o, gsub_ref[...]) * (1.0 - lam_init))
        tok = jnp.concatenate(toks, axis=-1)
        mo = _mem_attn(qm_scr[...], mk_ref[0].astype(BF16), mv_ref[0].astype(BF16), bd_ref[...],
                       gmq_ref[...], hmask_ref[...])
        out_ref[0] = (h_ref[0] + _dot(tok.astype(BF16), wout_ref[0:tw, :])
                      + _dot(mo.astype(BF16), wout_ref[tw:, :]))


def _layer_b_sample(h, cache_kt, cache_vh, page_table, k_new, v_new, mem_k, mem_v, g_mix, w_in, g_qn, lam_b,
                    g_subln, w_out, g_mqn, bd, hmask, qmask, bias, bias_new, lam_init, npg):
    b, sq, d = h.shape
    n_pages = page_table.shape[1]
    page = cache_kt.shape[2]
    tw = DIFF_HEADS * LANES
    mw = mem_k.shape[1]
    nm = mem_k.shape[2]
    rows = 2 * DIFF_HEADS * sq
    const = lambda i, j, pt: (0, 0)
    per_b = lambda i, j, pt: (i, 0, 0)

    def page_map(t, nd):
        return lambda i, j, pt: (pt[i, j * npg + t],) + (0,) * nd

    grid_spec = pltpu.PrefetchScalarGridSpec(
        num_scalar_prefetch=1,
        grid=(b, n_pages // npg),
        in_specs=[pl.BlockSpec((1, sq, d), per_b)]
        + [pl.BlockSpec((1, tw, page), page_map(t, 2)) for t in range(npg)]
        + [pl.BlockSpec((1, DIFF_HEADS, page, LANES), page_map(t, 3)) for t in range(npg)]
        + [
            pl.BlockSpec((1, sq, tw), per_b),
            pl.BlockSpec((1, sq, tw), per_b),
            pl.BlockSpec((1, mw, nm), per_b),
            pl.BlockSpec((1, mw, nm), per_b),
            pl.BlockSpec((1, d), const),
            pl.BlockSpec(w_in.shape, const),
            pl.BlockSpec((1, tw), const),
            pl.BlockSpec(lam_b.shape, const),
            pl.BlockSpec((1, LANES), const),
            pl.BlockSpec(w_out.shape, const),
            pl.BlockSpec((1, mw), const),
            pl.BlockSpec((256, 256), const),
            pl.BlockSpec(hmask.shape, const),
            pl.BlockSpec(qmask.shape, const),
            pl.BlockSpec(bias.shape, lambda i, j, pt: (0, 0, 0)),
            pl.BlockSpec(bias_new.shape, const),
        ],
        out_specs=pl.BlockSpec((1, sq, d), per_b),
        scratch_shapes=[pltpu.VMEM((rows, tw), BF16),
                        pltpu.VMEM((sq, mw), F32),
                        pltpu.VMEM((rows, 1), F32),
                        pltpu.VMEM((rows, 1), F32),
                        pltpu.VMEM((rows, LANES), F32),
                        pltpu.VMEM((2, page, tw), F32)],
    )
    return pl.pallas_call(
        functools.partial(_layer_b_sample_kernel, lam_init, npg),
        grid_spec=grid_spec,
        out_shape=jax.ShapeDtypeStruct((b, sq, d), F32),
        compiler_params=_cparams(2),
        name="layer_b_sample",
    )(page_table, h, *([cache_kt] * npg), *([cache_vh] * npg), k_new, v_new, mem_k, mem_v,
      g_mix.reshape(1, d), w_in.astype(BF16), jnp.tile(g_qn, 2 * DIFF_HEADS).reshape(1, tw), lam_b,
      g_subln.reshape(1, LANES), w_out.astype(BF16), jnp.tile(g_mqn, MEM_HEADS).reshape(1, mw), bd, hmask,
      qmask, bias, bias_new)


def _prompt_buckets(t):
    r = np.arange(t)[:, None]
    c = np.arange(t)[None, :]
    diag = np.where(c <= r, _bucket_np(r - c), -1)
    sub = _bucket_np(t + r - c)
    return np.stack([diag, sub]).astype(np.int32)


def _sample_buckets(sq, page):
    qi = np.tile(np.arange(sq), 2)[:, None]
    tok = np.arange(page)[None, :]
    far = np.full((2 * sq, page), N_BUCKETS - 1, np.int32)
    last = _bucket_np(page + qi - tok)
    new = np.where(tok <= qi, _bucket_np(qi - tok), -1)
    return np.stack([far, last, new]).astype(np.int32)


def _sample_qmask(sq):
    rows = np.arange(2 * DIFF_HEADS * sq)[:, None] // sq
    lanes = np.arange(DIFF_HEADS * LANES)[None, :] // GROUP
    return jnp.asarray((rows == lanes).astype(np.float32))


def _block(n, pref):
    return pref if n % pref == 0 else n


def kernel(x_prompt, x_sample, state_conv, cache_k, cache_v, cache_mem_k, cache_mem_v, page_table, mem_prompt, g_mix, g_ffn, w_in_a, w_conv, w_out_a, w_in_b, g_qn_b, lam_b, g_subln, w_out_b, g_kv, w_kv, g_kn, rel_bias, g_mem, w_mem_kv, g_mem_qn, g_mem_kn, w_peer_q, peer_subkeys, peer_u, peer_v):
    depth = g_mix.shape[0]
    n_a = w_in_a.shape[0]
    assert depth == 2 and n_a == 1, "one short-conv layer followed by one differential-attention layer"
    bp, sp, d = x_prompt.shape
    bs, ss, _ = x_sample.shape
    nm = mem_prompt.shape[1]
    mw = MEM_HEADS * GROUP
    tw = DIFF_HEADS * 2 * GROUP
    page = cache_k.shape[1]
    t_attn = _block(sp, 256)

    bd = _block_diag_mean()
    hmask = _mem_head_mask()
    cmask = _sub_head_mask()
    u_bf = peer_u.astype(BF16)
    vt_bf = jnp.swapaxes(peer_v.astype(BF16).reshape(depth, -1, PEER_EXPERT_CHUNK, d), 2, 3)

    def mem_t(x):
        return jnp.transpose(x, (0, 1, 3, 4, 2)).reshape(depth, x.shape[1], mw, nm)

    def mem_t_inv(x):
        return jnp.transpose(x.reshape(depth, x.shape[1], MEM_HEADS, GROUP, nm), (0, 1, 4, 2, 3))

    cache_kt = jnp.transpose(cache_k, (0, 2, 3, 4, 1)).reshape(cache_k.shape[0], tw, page)
    cache_vh = jnp.transpose(cache_v, (0, 2, 1, 3))

    mem_k_p, mem_v_p = _mem_kv(mem_prompt, g_mem, w_mem_kv, g_mem_kn, bd)
    mem_k_s = mem_t(cache_mem_k)
    mem_v_s = mem_t(cache_mem_v)

    bias_p = _rel_bias_tiles(rel_bias, _prompt_buckets(t_attn))
    n_pages = page_table.shape[1]
    npg = next(c for c in (8, 4, 2, 1) if n_pages % c == 0)
    bias_s = jnp.swapaxes(_rel_bias_tiles(rel_bias, _sample_buckets(ss, page)), 0, 1)
    bias_s = bias_s.reshape(3, DIFF_HEADS * 2 * ss, page)
    bias_steps = jnp.stack([jnp.zeros((bias_s.shape[1], npg * page), F32),
                            jnp.pad(bias_s[1], ((0, 0), ((npg - 1) * page, 0)))])
    lam_init = 0.8 - 0.6 * math.exp(-0.3 * 1)

    def ffn(h, l, tb):
        b, s, _ = h.shape
        return _peer_ffn(h.reshape(b * s, d), g_ffn[l], w_peer_q[l], peer_subkeys[l], u_bf, vt_bf, l,
                         tb, PEER_EXPERT_CHUNK).reshape(b, s, d)

    n = bp * sp
    tb = _block(n, 512)
    zeros_conv = jnp.zeros((bp, 2, w_conv.shape[-1]), F32)
    h, conv_p = _layer_a(x_prompt, zeros_conv, mem_k_p[0], mem_v_p[0], g_mix[0], w_in_a[0], w_conv[0],
                         w_out_a[0], g_mem_qn[0], bd, hmask, ts=_block(sp, 512))
    h = ffn(h, 0, tb)
    k_t, v_h, kt_bf, vh_bf = _shared_kv(h, g_kv, w_kv, g_kn, bd, t_attn, True)
    h = _layer_b_prompt(h, kt_bf, vh_bf, mem_k_p[1], mem_v_p[1], g_mix[1], w_in_b[0], g_qn_b[0], lam_b[0],
                        g_subln[0], w_out_b[0], g_mem_qn[1], bd, hmask, cmask, bias_p, lam_init, t_attn)
    y_p = ffn(h, 1, tb)
    k_p = jnp.transpose(k_t.reshape(bp, DIFF_HEADS, 2, GROUP, sp), (0, 4, 1, 2, 3))
    v_p = jnp.transpose(v_h, (0, 2, 1, 3))

    n = bs * ss
    tb = _block(n, 512)
    h, conv_s = _layer_a(x_sample, state_conv[0], mem_k_s[0], mem_v_s[0], g_mix[0], w_in_a[0], w_conv[0],
                         w_out_a[0], g_mem_qn[0], bd, hmask, ts=_block(ss, 512))
    h = ffn(h, 0, tb)
    k_s, v_s = _shared_kv(h.reshape(1, n, d), g_kv, w_kv, g_kn, bd, n, False)
    k_s = k_s.reshape(bs, ss, tw)
    v_s = v_s.reshape(bs, ss, tw)
    h = _layer_b_sample(h, cache_kt, cache_vh, page_table, k_s, v_s, mem_k_s[1], mem_v_s[1], g_mix[1],
                        w_in_b[0], g_qn_b[0], lam_b[0], g_subln[0], w_out_b[0], g_mem_qn[1], bd, hmask,
                        _sample_qmask(ss), bias_steps, bias_s[2], lam_init, npg)
    y_s = ffn(h, 1, tb)

    return (y_p, y_s, conv_p[None], conv_s[None], k_p, v_p,
            k_s.reshape(bs, ss, DIFF_HEADS, 2, GROUP), v_s.reshape(bs, ss, DIFF_HEADS, 2 * GROUP),
            mem_t_inv(mem_k_p), mem_t_inv(mem_v_p))
```

```python
import functools
import math

import numpy as np
import jax
import jax.numpy as jnp
from jax import lax
from jax.experimental import pallas as pl
from jax.experimental.pallas import tpu as pltpu

F32 = jnp.float32
BF16 = jnp.bfloat16

EPS = 1e-6
NEG = -1e30
LANES = 128
GROUP = 64
MEM_HEADS = 4
DIFF_HEADS = 6
N_BUCKETS = 32
MAX_DISTANCE = 128
PEER_HEADS = 8
PEER_KEYS = 128
PEER_TOPK = 16
PEER_EXPERT_CHUNK = 2048
VMEM_LIMIT = 56 * 1024 * 1024


def _cparams(n_grid):
    return pltpu.CompilerParams(dimension_semantics=("arbitrary",) * n_grid,
                                vmem_limit_bytes=VMEM_LIMIT)


def _dot(a, b):
    return jnp.dot(a, b, preferred_element_type=F32)


def _dot_nt(a, b):
    return lax.dot_general(a, b, (((1,), (1,)), ((), ())), preferred_element_type=F32)


def _rms(x, g):
    ms = jnp.mean(x * x, axis=-1, keepdims=True)
    return x * lax.rsqrt(ms + EPS) * g


def _group_rms(x, bd, g):
    x2 = x * x
    hi = x2.astype(BF16)
    lo = (x2 - hi.astype(F32)).astype(BF16)
    parts = []
    for s in range(x.shape[-1] // 256):
        sl = slice(s * 256, (s + 1) * 256)
        parts.append(_dot(hi[:, sl], bd) + _dot(lo[:, sl], bd))
    ms = parts[0] if len(parts) == 1 else jnp.concatenate(parts, axis=-1)
    return x * lax.rsqrt(ms + EPS) * g


def _mem_attn(qm, mk_t, mv_t, bd, gq, hmask):
    q = _group_rms(qm, bd, gq) * (GROUP ** -0.5)
    out = None
    for hd in range(MEM_HEADS):
        msk = hmask[hd:hd + 1, :]
        s = _dot((q * msk).astype(BF16), mk_t)
        m = jnp.max(s, axis=-1, keepdims=True)
        p = jnp.exp(s - m)
        l = jnp.sum(p, axis=-1, keepdims=True)
        t = _dot_nt(p.astype(BF16), mv_t) * (1.0 / l) * msk
        out = t if out is None else out + t
    return out


def _block_diag_mean():
    i = np.arange(256)
    return jnp.asarray((i[:, None] // GROUP == i[None, :] // GROUP).astype(np.float32) / GROUP, BF16)


def _mem_head_mask():
    i = np.arange(MEM_HEADS * GROUP)
    return jnp.asarray((i[None, :] // GROUP == np.arange(MEM_HEADS)[:, None]).astype(np.float32))


def _sub_head_mask():
    i = np.arange(2 * GROUP)
    return jnp.asarray((i[None, :] // GROUP == np.arange(2)[:, None]).astype(np.float32))


def _bucket_np(n):
    n = np.maximum(n, 0)
    max_exact = N_BUCKETS // 2
    ratio = np.log(np.maximum(n, 1).astype(np.float32) / np.float32(max_exact)) / np.float32(
        math.log(MAX_DISTANCE / max_exact))
    large = max_exact + (ratio * np.float32(N_BUCKETS - max_exact)).astype(np.int32)
    large = np.minimum(large, N_BUCKETS - 1)
    return np.where(n < max_exact, n, large).astype(np.int32)


def _mem_kv_kernel(mem_ref, g_ref, w_ref, gk_ref, bd_ref, k_ref, v_ref):
    xn = _rms(mem_ref[0], g_ref[0]).astype(BF16)
    z = _dot(xn, w_ref[0])
    w = z.shape[-1] // 2
    k_ref[0, 0] = _group_rms(z[:, :w], bd_ref[...], gk_ref[0]).T
    v_ref[0, 0] = z[:, w:].T


def _mem_kv(mem, g_mem, w_mem_kv, g_kn, bd):
    b, nm, d = mem.shape
    depth = g_mem.shape[0]
    w = w_mem_kv.shape[-1] // 2
    out = jax.ShapeDtypeStruct((depth, b, w, nm), F32)
    return pl.pallas_call(
        _mem_kv_kernel,
        grid=(depth, b),
        in_specs=[
            pl.BlockSpec((1, nm, d), lambda l, i: (i, 0, 0)),
            pl.BlockSpec((1, 1, d), lambda l, i: (l, 0, 0)),
            pl.BlockSpec((1, d, 2 * w), lambda l, i: (l, 0, 0)),
            pl.BlockSpec((1, 1, w), lambda l, i: (l, 0, 0)),
            pl.BlockSpec((256, 256), lambda l, i: (0, 0)),
        ],
        out_specs=[pl.BlockSpec((1, 1, w, nm), lambda l, i: (l, i, 0, 0))] * 2,
        out_shape=[out, out],
        compiler_params=_cparams(2),
        name="mem_kv",
    )(mem, g_mem.reshape(depth, 1, d), w_mem_kv.astype(BF16),
      jnp.tile(g_kn, (1, MEM_HEADS)).reshape(depth, 1, w), bd)


def _layer_a_kernel(h_ref, prev_ref, mk_ref, mv_ref, gmix_ref, win_ref, wconv_ref, wout_ref, gq_ref,
                    bd_ref, hmask_ref, out_ref, conv_ref, ubuf):
    j = pl.program_id(1)
    ts = h_ref.shape[1]
    tw = wconv_ref.shape[1]
    h = h_ref[0]
    xn = _rms(h, gmix_ref[...]).astype(BF16)
    z = _dot(xn, win_ref[...])
    u = z[:, tw:2 * tw] * z[:, 2 * tw:3 * tw]

    @pl.when(j == 0)
    def _():
        ubuf[6:8, :] = prev_ref[0]

    @pl.when(j > 0)
    def _():
        ubuf[6:8, :] = ubuf[ts + 6:ts + 8, :]

    ubuf[8:8 + ts, :] = u
    conv = (wconv_ref[0:1, :] * ubuf[6:6 + ts, :] + wconv_ref[1:2, :] * ubuf[7:7 + ts, :]
            + wconv_ref[2:3, :] * u)
    conv_ref[0] = ubuf[ts + 6:ts + 8, :]
    tok = z[:, :tw] * conv
    mo = _mem_attn(z[:, 3 * tw:], mk_ref[0].astype(BF16), mv_ref[0].astype(BF16), bd_ref[...],
                   gq_ref[...], hmask_ref[...])
    out_ref[0] = (h + _dot(tok.astype(BF16), wout_ref[0:tw, :])
                  + _dot(mo.astype(BF16), wout_ref[tw:, :]))


def _layer_a(h, conv_prev, mem_k, mem_v, g_mix, w_in, w_conv, w_out, g_qn, bd, hmask, ts):
    b, s, d = h.shape
    tw = w_conv.shape[1]
    mw = mem_k.shape[1]
    nm = mem_k.shape[2]
    const = lambda i, j: (0, 0)
    return pl.pallas_call(
        _layer_a_kernel,
        grid=(b, s // ts),
        in_specs=[
            pl.BlockSpec((1, ts, d), lambda i, j: (i, j, 0)),
            pl.BlockSpec((1, 2, tw), lambda i, j: (i, 0, 0)),
            pl.BlockSpec((1, mw, nm), lambda i, j: (i, 0, 0)),
            pl.BlockSpec((1, mw, nm), lambda i, j: (i, 0, 0)),
            pl.BlockSpec((1, d), const),
            pl.BlockSpec(w_in.shape, const),
            pl.BlockSpec(w_conv.shape, const),
            pl.BlockSpec(w_out.shape, const),
            pl.BlockSpec((1, mw), const),
            pl.BlockSpec((256, 256), const),
            pl.BlockSpec(hmask.shape, const),
        ],
        out_specs=[pl.BlockSpec((1, ts, d), lambda i, j: (i, j, 0)),
                   pl.BlockSpec((1, 2, tw), lambda i, j: (i, 0, 0))],
        out_shape=[jax.ShapeDtypeStruct((b, s, d), F32), jax.ShapeDtypeStruct((b, 2, tw), F32)],
        scratch_shapes=[pltpu.VMEM((ts + 8, tw), F32)],
        compiler_params=_cparams(2),
        name="layer_a",
    )(h, conv_prev, mem_k, mem_v, g_mix.reshape(1, d), w_in.astype(BF16), w_conv, w_out.astype(BF16),
      jnp.tile(g_qn, MEM_HEADS).reshape(1, mw), bd, hmask)


def _sort16_network():
    def merge(lo, hi, r):
        step = 2 * r
        if step < hi - lo:
            yield from merge(lo, hi, step)
            yield from merge(lo + r, hi, step)
            yield from ((i, i + r) for i in range(lo + r, hi - r, step))
        else:
            yield (lo, lo + r)

    def sort(lo, hi):
        if hi - lo >= 1:
            mid = lo + (hi - lo) // 2
            yield from sort(lo, mid)
            yield from sort(mid + 1, hi)
            yield from merge(lo, hi, 1)

    return tuple(sort(0, PEER_TOPK - 1))


_SORT16 = _sort16_network()


def _top16_sorted(v):
    x = [v[8 * k:8 * k + 8] for k in range(PEER_KEYS // 8)]

    def exchange(i, j):
        x[i], x[j] = jnp.maximum(x[i], x[j]), jnp.minimum(x[i], x[j])

    for i, j in _SORT16:
        exchange(i, j)
    for shift in (4, 2, 1):
        other = [pltpu.roll(xk, shift, axis=0) for xk in x]
        x = [jnp.maximum(x[k], other[PEER_TOPK - 1 - k]) for k in range(PEER_TOPK)]
        for dist in (8, 4, 2, 1):
            for k in range(PEER_TOPK):
                if k & dist == 0:
                    exchange(k, k + dist)
    return x


def _pack_rows(rows):
    sid = lax.broadcasted_iota(jnp.int32, (8, LANES), 0)
    out = rows[7]
    for r in range(6, -1, -1):
        out = jnp.where(sid == r, rows[r], out)
    return out


def _tile_rows(x, n):
    return jnp.concatenate([x] * (n // x.shape[0]), axis=0)


def _peer_pre_kernel(h_ref, g_ref, wq_ref, sk_ref, hnT_ref, k1_ref, e1_ref, kap_ref, e0_ref, sc_scr):
    tb = h_ref.shape[0]
    ncol = tb // LANES
    hn = _rms(h_ref[...], g_ref[...])
    hnT_ref[...] = hn.T.astype(BF16)
    q = _dot(hn.astype(BF16), wq_ref[...]).astype(BF16)
    for hc in range(2 * PEER_HEADS):
        sc_t = _dot_nt(sk_ref[hc % 2], q[:, hc * LANES:(hc + 1) * LANES])
        for col in range(ncol):
            sc_scr[hc, col] = sc_t[:, col * LANES:(col + 1) * LANES]

    def body(idx, carry):
        hd = idx // ncol
        col = idx % ncol
        s0 = sc_scr[2 * hd, col]
        s1 = sc_scr[2 * hd + 1, col]
        a = _top16_sorted(s0)
        b = _top16_sorted(s1)
        a_lo, a_hi = _pack_rows(a[0:8]), _pack_rows(a[8:16])
        b_lo, b_hi = _pack_rows(b[0:8]), _pack_rows(b[8:16])
        pad = jnp.full((PEER_KEYS - 10 * 8, LANES), -jnp.inf, F32)
        cand = jnp.concatenate([a[0] + b_lo, a[0] + b_hi] + [a[r] + b_lo for r in range(1, 8)] + [a_hi + b[0], pad],
                               axis=0)
        best = _top16_sorted(cand)
        t = best[PEER_TOPK - 1]
        zsum = None
        for s in best:
            e = jnp.exp(s - best[0])
            zsum = e if zsum is None else zsum + e
        zsum = _tile_rows(zsum, PEER_KEYS)
        kap = jnp.full(s0.shape, PEER_TOPK + 1.0, F32)
        for r in range(PEER_TOPK - 1, -1, -1):
            kap_r = jnp.full((8, LANES), PEER_TOPK + 1.0, F32)
            for c in range(PEER_TOPK // (r + 1)):
                kap_r = jnp.where(a[r] + b[c] >= t, float(PEER_TOPK - c), kap_r)
            kap = jnp.where(s0 >= _tile_rows(a[r], PEER_KEYS), _tile_rows(kap_r, PEER_KEYS), kap)
        k1 = jnp.zeros(s1.shape, F32)
        for c in range(PEER_TOPK - 1, -1, -1):
            k1 = jnp.where(s1 >= _tile_rows(b[c], PEER_KEYS), float(PEER_TOPK - c), k1)
        k1_ref[hd, col] = pltpu.bitcast(k1.astype(BF16), jnp.uint32)
        e1 = jnp.exp(s1 - _tile_rows(b[0], PEER_KEYS)) * (1.0 / zsum)
        e1_ref[hd, col] = pltpu.bitcast(e1.astype(BF16), jnp.uint32)
        kap_ref[hd, col] = kap
        e0_ref[hd, col] = jnp.exp(s0 - _tile_rows(a[0], PEER_KEYS))
        return carry

    lax.fori_loop(0, PEER_HEADS * ncol, body, 0, unroll=2)


def _packed_rows():
    return PEER_KEYS * jnp.dtype(BF16).itemsize // 4


def _peer_pre(h2d, g_ffn, w_q, sub_keys, tb):
    n, d = h2d.shape
    ncols = n // LANES
    ncol = tb // LANES
    const2 = lambda i: (0, 0)
    prow = _packed_rows()
    tiles = ([jax.ShapeDtypeStruct((PEER_HEADS, ncols, prow, LANES), jnp.uint32)] * 2
             + [jax.ShapeDtypeStruct((PEER_HEADS, ncols, PEER_KEYS, LANES), F32)] * 2)
    packed_spec = pl.BlockSpec((PEER_HEADS, ncol, prow, LANES), lambda i: (0, i, 0, 0))
    tile_spec = pl.BlockSpec((PEER_HEADS, ncol, PEER_KEYS, LANES), lambda i: (0, i, 0, 0))
    return pl.pallas_call(
        _peer_pre_kernel,
        grid=(n // tb,),
        in_specs=[
            pl.BlockSpec((tb, d), lambda i: (i, 0)),
            pl.BlockSpec((1, d), const2),
            pl.BlockSpec(w_q.shape, const2),
            pl.BlockSpec(sub_keys.shape, lambda i: (0, 0, 0)),
        ],
        out_specs=[pl.BlockSpec((d, tb), lambda i: (0, i))] + [packed_spec] * 2 + [tile_spec] * 2,
        out_shape=[jax.ShapeDtypeStruct((d, n), BF16)] + tiles,
        scratch_shapes=[pltpu.VMEM((2 * PEER_HEADS, ncol, PEER_KEYS, LANES), F32)],
        compiler_params=_cparams(1),
        name="peer_pre",
    )(h2d, g_ffn.reshape(1, d), w_q.astype(BF16), sub_keys.astype(BF16))


GATE_ROW_SPLIT = 2


def _row_tile_bf16(ref, hd, col, i, nrow):
    row = jnp.broadcast_to(ref[hd, col, pl.ds(i, 1), :], (16, LANES)).astype(BF16)
    return jnp.concatenate([row] * (nrow // 16), axis=0)


def _gate_slab(a_ref, p_ref, sub, i, k1_ref, e1_ref, kap_ref, e0_ref):
    sqrt_half = np.float32(np.sqrt(0.5))
    nrow = PEER_KEYS // GATE_ROW_SPLIT
    wrow = nrow * _packed_rows() // PEER_KEYS
    for col in range(a_ref.shape[2] // LANES):
        cs = slice(col * LANES, (col + 1) * LANES)
        kaps = [_row_tile_bf16(kap_ref, hd, col, i, nrow) for hd in range(PEER_HEADS)]
        e0s = [_row_tile_bf16(e0_ref, hd, col, i, nrow) for hd in range(PEER_HEADS)]
        for part in range(GATE_ROW_SPLIT):
            w = None
            for hd in range(PEER_HEADS):
                kap, e0 = kaps[hd], e0s[hd]
                k1 = pltpu.bitcast(k1_ref[hd, col, part * wrow:(part + 1) * wrow, :], BF16)
                e1 = pltpu.bitcast(e1_ref[hd, col, part * wrow:(part + 1) * wrow, :], BF16)
                t = jnp.where(k1 >= kap, e1, jnp.zeros((), BF16)) * e0
                w = t if w is None else w + t
            a = a_ref[sub, part * nrow:(part + 1) * nrow, cs]
            act = 0.5 * a * (1.0 + lax.erf(a * sqrt_half))
            p_ref[sub, part * nrow:(part + 1) * nrow, cs] = w * act.astype(BF16)


def _peer_dense_kernel(hnT_ref, k1_ref, e1_ref, kap_ref, e0_ref, u_ref, vt_ref, h_ref, out_ref, acc, a_scr, p_scr):
    k = pl.program_id(1)
    nsub = a_scr.shape[0]

    @pl.when(k == 0)
    def _():
        acc[...] = jnp.zeros_like(acc)

    a_scr[...] = _dot(u_ref[...], hnT_ref[...]).reshape(a_scr.shape)

    def body(sub, carry):
        _gate_slab(a_scr, p_scr, sub, k * nsub + sub, k1_ref, e1_ref, kap_ref, e0_ref)
        return carry

    lax.fori_loop(0, nsub, body, 0)
    acc[...] += _dot(vt_ref[...], p_scr[...].reshape(nsub * LANES, p_scr.shape[2]))

    @pl.when(k == pl.num_programs(1) - 1)
    def _():
        out_ref[...] = h_ref[...] + acc[...].T


def _peer_dense(h2d, hnT, k1, e1, kap, e0, u_bf, vt_bf, layer, tb, ec):
    n, d = h2d.shape
    ne = u_bf.shape[1]
    ncol = tb // LANES
    packed_spec = pl.BlockSpec((PEER_HEADS, ncol, _packed_rows(), LANES), lambda t, k: (0, t, 0, 0))
    tile_spec = pl.BlockSpec((PEER_HEADS, ncol, PEER_KEYS, LANES), lambda t, k: (0, t, 0, 0))
    return pl.pallas_call(
        _peer_dense_kernel,
        grid=(n // tb, ne // ec),
        in_specs=[pl.BlockSpec((d, tb), lambda t, k: (0, t))] + [packed_spec] * 2 + [tile_spec] * 2 + [
            pl.BlockSpec((None, ec, d), lambda t, k: (layer, k, 0)),
            pl.BlockSpec((None, d, ec), lambda t, k: (layer, 0, k)),
            pl.BlockSpec((tb, d), lambda t, k: (t, 0)),
        ],
        out_specs=pl.BlockSpec((tb, d), lambda t, k: (t, 0)),
        out_shape=jax.ShapeDtypeStruct((n, d), F32),
        scratch_shapes=[pltpu.VMEM((d, tb), F32),
                        pltpu.VMEM((ec // LANES, LANES, tb), F32),
                        pltpu.VMEM((ec // LANES, LANES, tb), BF16)],
        compiler_params=_cparams(2),
        name="peer_dense",
    )(hnT, k1, e1, kap, e0, u_bf, vt_bf, h2d)


def _peer_ffn(h2d, g_ffn, w_q, sub_keys, u_bf, vt_bf, layer, tb, ec):
    hnT, k1, e1, kap, e0 = _peer_pre(h2d, g_ffn, w_q, sub_keys, tb)
    return _peer_dense(h2d, hnT, k1, e1, kap, e0, u_bf, vt_bf, layer, tb, ec)


def _shared_kv_kernel(head_major, h_ref, g_ref, w_ref, gk_ref, bd_ref, k_ref, v_ref, *bf_refs):
    tw = gk_ref.shape[-1]
    xn = _rms(h_ref[0], g_ref[...]).astype(BF16)
    z = _dot(xn, w_ref[...])
    k = _group_rms(z[:, :tw], bd_ref[...], gk_ref[...])
    v = z[:, tw:]
    if not head_major:
        k_ref[0] = k
        v_ref[0] = v
        return
    kt_ref, vh_ref = bf_refs
    k_t = k.T
    k_ref[0] = k_t
    for hd in range(DIFF_HEADS):
        sl = slice(hd * LANES, (hd + 1) * LANES)
        kt_ref[0, hd] = k_t[sl, :].astype(BF16)
        v_ref[0, hd] = v[:, sl]
        vh_ref[0, hd] = v[:, sl].astype(BF16)


def _shared_kv(h, g_kv, w_kv, g_kn, bd, ts, head_major):
    b, s, d = h.shape
    tw = DIFF_HEADS * 2 * GROUP
    const = lambda i, j: (0, 0)
    if head_major:
        out_specs = [pl.BlockSpec((1, tw, ts), lambda i, j: (i, 0, j)),
                     pl.BlockSpec((1, DIFF_HEADS, ts, LANES), lambda i, j: (i, 0, j, 0)),
                     pl.BlockSpec((1, DIFF_HEADS, LANES, ts), lambda i, j: (i, 0, 0, j)),
                     pl.BlockSpec((1, DIFF_HEADS, ts, LANES), lambda i, j: (i, 0, j, 0))]
        out_shape = [jax.ShapeDtypeStruct((b, tw, s), F32),
                     jax.ShapeDtypeStruct((b, DIFF_HEADS, s, LANES), F32),
                     jax.ShapeDtypeStruct((b, DIFF_HEADS, LANES, s), BF16),
                     jax.ShapeDtypeStruct((b, DIFF_HEADS, s, LANES), BF16)]
    else:
        out_specs = [pl.BlockSpec((1, ts, tw), lambda i, j: (i, j, 0))] * 2
        out_shape = [jax.ShapeDtypeStruct((b, s, tw), F32)] * 2
    return pl.pallas_call(
        functools.partial(_shared_kv_kernel, head_major),
        grid=(b, s // ts),
        in_specs=[
            pl.BlockSpec((1, ts, d), lambda i, j: (i, j, 0)),
            pl.BlockSpec((1, d), const),
            pl.BlockSpec(w_kv.shape, const),
            pl.BlockSpec((1, tw), const),
            pl.BlockSpec((256, 256), const),
        ],
        out_specs=out_specs,
        out_shape=out_shape,
        compiler_params=_cparams(2),
        name="shared_kv",
    )(h, g_kv.reshape(1, d), w_kv.astype(BF16), jnp.tile(g_kn, 2 * DIFF_HEADS).reshape(1, tw), bd)


def _rel_bias_kernel(rb_ref, bucket_ref, out_ref):
    hd = pl.program_id(0)
    bk = bucket_ref[0]
    far = rb_ref[N_BUCKETS - 1, hd]
    acc = jnp.zeros(bk.shape, F32)
    for n in range(N_BUCKETS - 1):
        acc = jnp.where(bk == n, rb_ref[n, hd] - far, acc)
    out_ref[0, 0] = jnp.where(bk < 0, NEG, acc)


def _rel_bias_tiles(rel_bias, buckets):
    kinds, r, c = buckets.shape
    return pl.pallas_call(
        _rel_bias_kernel,
        grid=(DIFF_HEADS, kinds),
        in_specs=[pl.BlockSpec(memory_space=pltpu.SMEM),
                  pl.BlockSpec((1, r, c), lambda h, k: (k, 0, 0))],
        out_specs=pl.BlockSpec((1, 1, r, c), lambda h, k: (h, k, 0, 0)),
        out_shape=jax.ShapeDtypeStruct((DIFF_HEADS, kinds, r, c), F32),
        compiler_params=_cparams(2),
        name="rel_bias_tiles",
    )(rel_bias, jnp.asarray(buckets))


def _lambda_value(lam_ref, lam_init):
    lp = lam_ref[...]
    s01 = jnp.sum(lp[0:1] * lp[1:2], axis=-1, keepdims=True)
    s23 = jnp.sum(lp[2:3] * lp[3:4], axis=-1, keepdims=True)
    return jnp.exp(s01) - jnp.exp(s23) + lam_init


def _layer_b_prompt_kernel(lam_init, h_ref, kt_ref, vh_ref, mk_ref, mv_ref, gmix_ref, win_ref, gq_ref,
                           lam_ref, gsub_ref, wout_ref, gmq_ref, bd_ref, hmask_ref, cmask_ref, bias_ref,
                           out_ref, q_scr, o_scr):
    qb = pl.program_id(1)
    t = h_ref.shape[1]
    tw = DIFF_HEADS * LANES
    h = h_ref[0]
    xn = _rms(h, gmix_ref[...]).astype(BF16)
    z = _dot(xn, win_ref[...])
    q = _group_rms(z[:, :tw], bd_ref[...], gq_ref[...]) * (GROUP ** -0.5)
    for hd in range(DIFF_HEADS):
        qh = q[:, hd * LANES:(hd + 1) * LANES]
        for c in range(2):
            q_scr[2 * hd + c] = (qh * cmask_ref[c:c + 1, :]).astype(BF16)
    lam = _lambda_value(lam_ref, lam_init)

    def attend(nk):
        w = nk * t

        def head_body(hd, carry):
            q2 = q_scr[pl.ds(2 * hd, 2)].reshape(2 * t, LANES)
            s = _dot(q2, kt_ref[0, hd, :, 0:w])
            near = [s[:, w - t:] + jnp.concatenate([bias_ref[hd, 0]] * 2, axis=0)]
            if nk > 1:
                near = [s[:, w - 2 * t:w - t] + jnp.concatenate([bias_ref[hd, 1]] * 2, axis=0)] + near
            if nk > 2:
                near = [s[:, :w - 2 * t]] + near
            s = near[0] if len(near) == 1 else jnp.concatenate(near, axis=-1)
            p = jnp.exp(s - jnp.max(s, axis=-1, keepdims=True))
            l = jnp.sum(p, axis=-1, keepdims=True)
            outs = _dot(p.astype(BF16), vh_ref[0, hd, 0:w, :]) * (1.0 / l)
            o = outs[0:t] - lam * outs[t:2 * t]
            o_scr[hd] = _rms(o, gsub_ref[...]) * (1.0 - lam_init)
            return carry

        lax.fori_loop(0, DIFF_HEADS, head_body, 0)

    for nk in range(1, kt_ref.shape[3] // t + 1):
        pl.when(qb == nk - 1)(functools.partial(attend, nk))

    tok = jnp.concatenate([o_scr[hd] for hd in range(DIFF_HEADS)], axis=-1)
    mo = _mem_attn(z[:, tw:], mk_ref[0].astype(BF16), mv_ref[0].astype(BF16), bd_ref[...],
                   gmq_ref[...], hmask_ref[...])
    out_ref[0] = (h + _dot(tok.astype(BF16), wout_ref[0:tw, :])
                  + _dot(mo.astype(BF16), wout_ref[tw:, :]))


def _layer_b_prompt(h, kt, vh, mem_k, mem_v, g_mix, w_in, g_qn, lam_b, g_subln, w_out, g_mqn, bd, hmask,
                    cmask, bias, lam_init, t):
    b, s, d = h.shape
    tw = DIFF_HEADS * LANES
    mw = mem_k.shape[1]
    nm = mem_k.shape[2]
    const = lambda i, j: (0, 0)
    return pl.pallas_call(
        functools.partial(_layer_b_prompt_kernel, lam_init),
        grid=(b, s // t),
        in_specs=[
            pl.BlockSpec((1, t, d), lambda i, j: (i, j, 0)),
            pl.BlockSpec((1,) + kt.shape[1:], lambda i, j: (i, 0, 0, 0)),
            pl.BlockSpec((1,) + vh.shape[1:], lambda i, j: (i, 0, 0, 0)),
            pl.BlockSpec((1, mw, nm), lambda i, j: (i, 0, 0)),
            pl.BlockSpec((1, mw, nm), lambda i, j: (i, 0, 0)),
            pl.BlockSpec((1, d), const),
            pl.BlockSpec(w_in.shape, const),
            pl.BlockSpec((1, tw), const),
            pl.BlockSpec(lam_b.shape, const),
            pl.BlockSpec((1, LANES), const),
            pl.BlockSpec(w_out.shape, const),
            pl.BlockSpec((1, mw), const),
            pl.BlockSpec((256, 256), const),
            pl.BlockSpec(hmask.shape, const),
            pl.BlockSpec(cmask.shape, const),
            pl.BlockSpec(bias.shape, lambda i, j: (0, 0, 0, 0)),
        ],
        out_specs=pl.BlockSpec((1, t, d), lambda i, j: (i, j, 0)),
        out_shape=jax.ShapeDtypeStruct((b, s, d), F32),
        scratch_shapes=[pltpu.VMEM((2 * DIFF_HEADS, t, LANES), BF16),
                        pltpu.VMEM((DIFF_HEADS, t, LANES), F32)],
        compiler_params=_cparams(2),
        name="layer_b_prompt",
    )(h, kt, vh, mem_k, mem_v, g_mix.reshape(1, d), w_in.astype(BF16),
      jnp.tile(g_qn, 2 * DIFF_HEADS).reshape(1, tw), lam_b, g_subln.reshape(1, LANES),
      w_out.astype(BF16), jnp.tile(g_mqn, MEM_HEADS).reshape(1, mw), bd, hmask, cmask, bias)


def _layer_b_sample_kernel(lam_init, npg, pt_ref, h_ref, *refs):
    del pt_ref
    ck_refs, cv_refs = refs[:npg], refs[npg:2 * npg]
    (kn_ref, vn_ref, mk_ref, mv_ref, gmix_ref, win_ref, gq_ref, lam_ref, gsub_ref, wout_ref, gmq_ref, bd_ref,
     hmask_ref, qmask_ref, bias_ref, bias_new_ref, out_ref, qbd, qm_scr, m_scr, l_scr, acc, kv_scr) = refs[2 * npg:]
    p = pl.program_id(1)
    last = pl.num_programs(1) - 1
    sq = h_ref.shape[1]
    tw = DIFF_HEADS * LANES
    rows = qbd.shape[0]

    @pl.when(p == 0)
    def _():
        xn = _rms(h_ref[0], gmix_ref[...]).astype(BF16)
        z = _dot(xn, win_ref[...])
        q = _group_rms(z[:, :tw], bd_ref[...], gq_ref[...]) * (GROUP ** -0.5)
        qbd[...] = (jnp.concatenate([q] * (rows // sq), axis=0) * qmask_ref[...]).astype(BF16)
        qm_scr[...] = z[:, tw:]
        m_scr[...] = jnp.full(m_scr.shape, NEG, F32)
        l_scr[...] = jnp.zeros_like(l_scr)
        acc[...] = jnp.zeros_like(acc)

    def update(s, v_heads):
        m = m_scr[...]
        m_new = jnp.maximum(m, jnp.max(s, axis=-1, keepdims=True))
        alpha = jnp.exp(m - m_new)
        pr = jnp.exp(s - m_new)
        l_scr[...] = alpha * l_scr[...] + jnp.sum(pr, axis=-1, keepdims=True)
        pr = pr.astype(BF16)
        pv = [_dot(pr[hd * 2 * sq:(hd + 1) * 2 * sq], v_heads[hd]) for hd in range(DIFF_HEADS)]
        acc[...] = alpha * acc[...] + jnp.concatenate(pv, axis=0)
        m_scr[...] = m_new

    kind = jnp.where(p == last, 1, 0)
    kt = jnp.concatenate([ck_refs[t][0].astype(BF16) for t in range(npg)], axis=1)
    update(_dot(qbd[...], kt) + bias_ref[kind],
           [jnp.concatenate([cv_refs[t][0, hd].astype(BF16) for t in range(npg)], axis=0)
            for hd in range(DIFF_HEADS)])

    @pl.when(p == last)
    def _():
        kv_scr[...] = jnp.zeros_like(kv_scr)
        kv_scr[0, 0:sq, :] = kn_ref[0]
        kv_scr[1, 0:sq, :] = vn_ref[0]
        v_new = kv_scr[1].astype(BF16)
        update(_dot_nt(qbd[...], kv_scr[0].astype(BF16)) + bias_new_ref[...],
               [v_new[:, hd * LANES:(hd + 1) * LANES] for hd in range(DIFF_HEADS)])
        lam = _lambda_value(lam_ref, lam_init)
        o_all = acc[...] * (1.0 / l_scr[...])
        toks = []
        for hd in range(DIFF_HEADS):
            r0 = hd * 2 * sq
            o = o_all[r0:r0 + sq] - lam * o_all[r0 + sq:r0 + 2 * sq]
            toks.append(_rms(o, gsub_ref[...]) * (1.0 - lam_init))
        tok = jnp.concatenate(toks, axis=-1)
        mo = _mem_attn(qm_scr[...], mk_ref[0].astype(BF16), mv_ref[0].astype(BF16), bd_ref[...],
                       gmq_ref[...], hmask_ref[...])
        out_ref[0] = (h_ref[0] + _dot(tok.astype(BF16), wout_ref[0:tw, :])
                      + _dot(mo.astype(BF16), wout_ref[tw:, :]))


def _layer_b_sample(h, cache_kt, cache_vh, page_table, k_new, v_new, mem_k, mem_v, g_mix, w_in, g_qn, lam_b,
                    g_subln, w_out, g_mqn, bd, hmask, qmask, bias, bias_new, lam_init, npg):
    b, sq, d = h.shape
    n_pages = page_table.shape[1]
    page = cache_kt.shape[2]
    tw = DIFF_HEADS * LANES
    mw = mem_k.shape[1]
    nm = mem_k.shape[2]
    rows = 2 * DIFF_HEADS * sq
    const = lambda i, j, pt: (0, 0)
    per_b = lambda i, j, pt: (i, 0, 0)

    def page_map(t, nd):
        return lambda i, j, pt: (pt[i, j * npg + t],) + (0,) * nd

    grid_spec = pltpu.PrefetchScalarGridSpec(
        num_scalar_prefetch=1,
        grid=(b, n_pages // npg),
        in_specs=[pl.BlockSpec((1, sq, d), per_b)]
        + [pl.BlockSpec((1, tw, page), page_map(t, 2)) for t in range(npg)]
        + [pl.BlockSpec((1, DIFF_HEADS, page, LANES), page_map(t, 3)) for t in range(npg)]
        + [
            pl.BlockSpec((1, sq, tw), per_b),
            pl.BlockSpec((1, sq, tw), per_b),
            pl.BlockSpec((1, mw, nm), per_b),
            pl.BlockSpec((1, mw, nm), per_b),
            pl.BlockSpec((1, d), const),
            pl.BlockSpec(w_in.shape, const),
            pl.BlockSpec((1, tw), const),
            pl.BlockSpec(lam_b.shape, const),
            pl.BlockSpec((1, LANES), const),
            pl.BlockSpec(w_out.shape, const),
            pl.BlockSpec((1, mw), const),
            pl.BlockSpec((256, 256), const),
            pl.BlockSpec(hmask.shape, const),
            pl.BlockSpec(qmask.shape, const),
            pl.BlockSpec(bias.shape, lambda i, j, pt: (0, 0, 0)),
            pl.BlockSpec(bias_new.shape, const),
        ],
        out_specs=pl.BlockSpec((1, sq, d), per_b),
        scratch_shapes=[pltpu.VMEM((rows, tw), BF16),
                        pltpu.VMEM((sq, mw), F32),
                        pltpu.VMEM((rows, 1), F32),
                        pltpu.VMEM((rows, 1), F32),
                        pltpu.VMEM((rows, LANES), F32),
                        pltpu.VMEM((2, page, tw), F32)],
    )
    return pl.pallas_call(
        functools.partial(_layer_b_sample_kernel, lam_init, npg),
        grid_spec=grid_spec,
        out_shape=jax.ShapeDtypeStruct((b, sq, d), F32),
        compiler_params=_cparams(2),
        name="layer_b_sample",
    )(page_table, h, *([cache_kt] * npg), *([cache_vh] * npg), k_new, v_new, mem_k, mem_v,
      g_mix.reshape(1, d), w_in.astype(BF16), jnp.tile(g_qn, 2 * DIFF_HEADS).reshape(1, tw), lam_b,
      g_subln.reshape(1, LANES), w_out.astype(BF16), jnp.tile(g_mqn, MEM_HEADS).reshape(1, mw), bd, hmask,
      qmask, bias, bias_new)


def _prompt_buckets(t):
    r = np.arange(t)[:, None]
    c = np.arange(t)[None, :]
    diag = np.where(c <= r, _bucket_np(r - c), -1)
    sub = _bucket_np(t + r - c)
    return np.stack([diag, sub]).astype(np.int32)


def _sample_buckets(sq, page):
    qi = np.tile(np.arange(sq), 2)[:, None]
    tok = np.arange(page)[None, :]
    far = np.full((2 * sq, page), N_BUCKETS - 1, np.int32)
    last = _bucket_np(page + qi - tok)
    new = np.where(tok <= qi, _bucket_np(qi - tok), -1)
    return np.stack([far, last, new]).astype(np.int32)


def _sample_qmask(sq):
    rows = np.arange(2 * DIFF_HEADS * sq)[:, None] // sq
    lanes = np.arange(DIFF_HEADS * LANES)[None, :] // GROUP
    return jnp.asarray((rows == lanes).astype(np.float32))


def _block(n, pref):
    return pref if n % pref == 0 else n


def kernel(x_prompt, x_sample, state_conv, cache_k, cache_v, cache_mem_k, cache_mem_v, page_table, mem_prompt, g_mix, g_ffn, w_in_a, w_conv, w_out_a, w_in_b, g_qn_b, lam_b, g_subln, w_out_b, g_kv, w_kv, g_kn, rel_bias, g_mem, w_mem_kv, g_mem_qn, g_mem_kn, w_peer_q, peer_subkeys, peer_u, peer_v):
    depth = g_mix.shape[0]
    n_a = w_in_a.shape[0]
    assert depth == 2 and n_a == 1, "one short-conv layer followed by one differential-attention layer"
    bp, sp, d = x_prompt.shape
    bs, ss, _ = x_sample.shape
    nm = mem_prompt.shape[1]
    mw = MEM_HEADS * GROUP
    tw = DIFF_HEADS * 2 * GROUP
    page = cache_k.shape[1]
    t_attn = _block(sp, 256)

    bd = _block_diag_mean()
    hmask = _mem_head_mask()
    cmask = _sub_head_mask()
    u_bf = peer_u.astype(BF16)
    vt_bf = jnp.swapaxes(peer_v.astype(BF16), 1, 2)

    def mem_t(x):
        return jnp.transpose(x, (0, 1, 3, 4, 2)).reshape(depth, x.shape[1], mw, nm)

    def mem_t_inv(x):
        return jnp.transpose(x.reshape(depth, x.shape[1], MEM_HEADS, GROUP, nm), (0, 1, 4, 2, 3))

    cache_kt = jnp.transpose(cache_k, (0, 2, 3, 4, 1)).reshape(cache_k.shape[0], tw, page)
    cache_vh = jnp.transpose(cache_v, (0, 2, 1, 3))

    mem_k_p, mem_v_p = _mem_kv(mem_prompt, g_mem, w_mem_kv, g_mem_kn, bd)
    mem_k_s = mem_t(cache_mem_k)
    mem_v_s = mem_t(cache_mem_v)

    bias_p = _rel_bias_tiles(rel_bias, _prompt_buckets(t_attn))
    n_pages = page_table.shape[1]
    npg = next(c for c in (8, 4, 2, 1) if n_pages % c == 0)
    bias_s = jnp.swapaxes(_rel_bias_tiles(rel_bias, _sample_buckets(ss, page)), 0, 1)
    bias_s = bias_s.reshape(3, DIFF_HEADS * 2 * ss, page)
    bias_steps = jnp.stack([jnp.zeros((bias_s.shape[1], npg * page), F32),
                            jnp.pad(bias_s[1], ((0, 0), ((npg - 1) * page, 0)))])
    lam_init = 0.8 - 0.6 * math.exp(-0.3 * 1)

    def ffn(h, l, tb):
        b, s, _ = h.shape
        return _peer_ffn(h.reshape(b * s, d), g_ffn[l], w_peer_q[l], peer_subkeys[l], u_bf, vt_bf, l,
                         tb, PEER_EXPERT_CHUNK).reshape(b, s, d)

    n = bp * sp
    tb = _block(n, 512)
    zeros_conv = jnp.zeros((bp, 2, w_conv.shape[-1]), F32)
    h, conv_p = _layer_a(x_prompt, zeros_conv, mem_k_p[0], mem_v_p[0], g_mix[0], w_in_a[0], w_conv[0],
                         w_out_a[0], g_mem_qn[0], bd, hmask, ts=_block(sp, 512))
    h = ffn(h, 0, tb)
    k_t, v_h, kt_bf, vh_bf = _shared_kv(h, g_kv, w_kv, g_kn, bd, t_attn, True)
    h = _layer_b_prompt(h, kt_bf, vh_bf, mem_k_p[1], mem_v_p[1], g_mix[1], w_in_b[0], g_qn_b[0], lam_b[0],
                        g_subln[0], w_out_b[0], g_mem_qn[1], bd, hmask, cmask, bias_p, lam_init, t_attn)
    y_p = ffn(h, 1, tb)
    k_p = jnp.transpose(k_t.reshape(bp, DIFF_HEADS, 2, GROUP, sp), (0, 4, 1, 2, 3))
    v_p = jnp.transpose(v_h, (0, 2, 1, 3))

    n = bs * ss
    tb = _block(n, 512)
    h, conv_s = _layer_a(x_sample, state_conv[0], mem_k_s[0], mem_v_s[0], g_mix[0], w_in_a[0], w_conv[0],
                         w_out_a[0], g_mem_qn[0], bd, hmask, ts=_block(ss, 512))
    h = ffn(h, 0, tb)
    k_s, v_s = _shared_kv(h.reshape(1, n, d), g_kv, w_kv, g_kn, bd, n, False)
    k_s = k_s.reshape(bs, ss, tw)
    v_s = v_s.reshape(bs, ss, tw)
    h = _layer_b_sample(h, cache_kt, cache_vh, page_table, k_s, v_s, mem_k_s[1], mem_v_s[1], g_mix[1],
                        w_in_b[0], g_qn_b[0], lam_b[0], g_subln[0], w_out_b[0], g_mem_qn[1], bd, hmask,
                        _sample_qmask(ss), bias_steps, bias_s[2], lam_init, npg)
    y_s = ffn(h, 1, tb)

    return (y_p, y_s, conv_p[None], conv_s[None], k_p, v_p,
            k_s.reshape(bs, ss, DIFF_HEADS, 2, GROUP), v_s.reshape(bs, ss, DIFF_HEADS, 2 * GROUP),
            mem_t_inv(mem_k_p), mem_t_inv(mem_v_p))
```
